```python
import math
import jax
import jax.numpy as jnp
from jax import lax
import numpy as np

D_MODEL = 1024
BATCH = 16
SEQ = 4096
DEPTH = 4
DEC_BATCH = 16
DEC_SEQ = 16
PAST_LEN = 4096

CHUNK = 64
N_EVEN = (DEPTH + 1) // 2
N_ODD = DEPTH // 2
A_HEADS = 8
A_HEAD_DIM = 64
A_WIDTH = A_HEADS * A_HEAD_DIM
DECAY_LORA = 32
ICLR_LORA = 32
GATE_LORA = 96
A_COLS = 3 * A_WIDTH + DECAY_LORA + ICLR_LORA + GATE_LORA
B_HEADS = 4
B_KEY_DIM = 64
B_VAL_DIM = 128
B_KEY_WIDTH = B_HEADS * B_KEY_DIM
B_WIDTH = B_HEADS * B_VAL_DIM
GLA_RANK = 16
GLA_TAU = 16.0
B_COLS = 2 * B_KEY_WIDTH + 2 * B_WIDTH + GLA_RANK
AB_COLS = A_COLS + B_COLS
C_HEADS = 16
C_HEAD_DIM = 64
C_WIDTH = C_HEADS * C_HEAD_DIM
IDX_HEADS = 8
IDX_DIM = 64
C_COLS = 3 * C_WIDTH + IDX_HEADS * IDX_DIM + IDX_DIM + IDX_HEADS
TOPK_MAX = 256
QUERY_BLOCK = 128
REL_BUCKETS = 32
REL_MAX_DIST = 256
D_FF = 2816
CONV_W = 3
N_MOD = 6
RMS_EPS = 1e-6
GN_EPS = 64e-5
NEG_INF = -1e30

kernel_name = 'hybrid_stream_rwkv7_gla_dsa_step'


def rmsnorm(x, gain):
    xf = x.astype(jnp.float32)
    y = xf * lax.rsqrt(jnp.mean(xf * xf, axis=-1, keepdims=True) + RMS_EPS)
    return (y * gain.astype(jnp.float32)).astype(x.dtype)


def modulate(x, gain, shift, scale):
    return rmsnorm(x, gain) * (1.0 + scale[:, None, :]) + shift[:, None, :]


def head_groupnorm(y, gain, bias):
    mu = jnp.mean(y, axis=-1, keepdims=True)
    var = jnp.mean(jnp.square(y - mu), axis=-1, keepdims=True)
    yn = (y - mu) * lax.rsqrt(var + GN_EPS)
    return yn.reshape(y.shape[0], y.shape[1], -1) * gain + bias


def head_rmsnorm(y, gain):
    yn = y * lax.rsqrt(jnp.mean(jnp.square(y), axis=-1, keepdims=True) + RMS_EPS)
    return yn.reshape(y.shape[0], y.shape[1], -1) * gain


def token_shift(p, buf, mu):
    p_pad = jnp.concatenate([buf.astype(p.dtype), p], axis=1)
    return p + (p_pad[:, :-1] - p) * mu, p_pad[:, -1:]


def rwkv7_scan(r, w, k, v, kk, a, s0):
    def step(s, inp):
        r_t, w_t, k_t, v_t, kk_t, a_t = inp
        sa = jnp.einsum('bhvk,bhk->bhv', s, kk_t)
        s = s * w_t[:, :, None, :] - sa[..., None] * (kk_t * a_t)[:, :, None, :] + v_t[..., None] * k_t[:, :, None, :]
        return s, jnp.einsum('bhvk,bhk->bhv', s, r_t)
    xs = tuple(jnp.moveaxis(z, 1, 0) for z in (r, w, k, v, kk, a))
    s_final, y = lax.scan(step, s0, xs)
    return jnp.moveaxis(y, 0, 1), s_final


def gla_chunked(q, k, v, log_a, s0):
    bsz, t_len, heads, dk = q.shape
    dv = v.shape[-1]
    c = min(CHUNK, t_len)
    n = t_len // c
    causal = jnp.tril(jnp.ones((c, c), dtype=bool))[None, :, :, None, None]

    def chunks(z):
        return jnp.moveaxis(z.reshape(bsz, n, c, *z.shape[2:]), 1, 0)

    def step(s, inp):
        qc, kc, vc, gc = inp
        b = jnp.cumsum(gc, axis=1)
        b_last = b[:, -1]
        inter = jnp.einsum('bthk,bhkv->bthv', qc * jnp.exp(b), s)
        decay = jnp.where(causal, jnp.exp(jnp.minimum(b[:, :, None] - b[:, None], 0.0)), 0.0)
        att = jnp.einsum('bthk,bshk,btshk->bths', qc, kc, decay)
        intra = jnp.einsum('bths,bshv->bthv', att, vc)
        s = jnp.exp(b_last)[..., None] * s + jnp.einsum('bshk,bshv->bhkv', kc * jnp.exp(b_last[:, None] - b), vc)
        return s, inter + intra

    s_final, o = lax.scan(step, s0, tuple(chunks(z) for z in (q, k, v, log_a)))
    return jnp.moveaxis(o, 0, 1).reshape(bsz, t_len, heads, dv), s_final


def rel_bucket(rel):
    half = REL_BUCKETS // 2
    max_exact = half // 2
    n = jnp.abs(rel)
    nf = jnp.maximum(n, 1).astype(jnp.float32)
    large = max_exact + (jnp.log(nf / max_exact) / math.log(REL_MAX_DIST / max_exact) * (half - max_exact)).astype(jnp.int32)
    large = jnp.minimum(large, half - 1)
    return jnp.where(rel > 0, half, 0) + jnp.where(n < max_exact, n, large)


def dsa_attention(q, qi, wi, k_all, v_all, ki_all, rel_bias, past):
    bsz, t_len, heads, dh = q.shape
    l_len = k_all.shape[1]
    topk = min(TOPK_MAX, l_len // 4)
    qb = min(QUERY_BLOCK, t_len)
    nb = t_len // qb
    k_pos = jnp.arange(l_len)

    def blocks(z):
        return z.reshape(bsz * nb, qb, *z.shape[2:])

    b_idx = jnp.repeat(jnp.arange(bsz), nb)
    q_start = jnp.tile(jnp.arange(nb) * qb, bsz) + past

    def one_block(inp):
        q_b, qi_b, wi_b, bi, q0 = inp
        q_pos = q0 + jnp.arange(qb)
        s_idx = jnp.einsum('qhd,sd->qhs', qi_b, ki_all[bi]).astype(jnp.float32) * IDX_DIM ** -0.5
        score = jnp.einsum('qh,qhs->qs', wi_b.astype(jnp.float32) * IDX_HEADS ** -0.5, jax.nn.relu(s_idx))
        admissible = (k_pos // CHUNK)[None, :] <= (q_pos // CHUNK)[:, None]
        score = jnp.where(admissible, score, NEG_INF)
        _, sel = lax.top_k(score, topk)
        valid = jnp.take_along_axis(admissible, sel, axis=1)
        k_sel = k_all[bi, sel]
        v_sel = v_all[bi, sel]
        logits = jnp.einsum('qhd,qjhd->qhj', q_b, k_sel).astype(jnp.float32) * dh ** -0.5
        bias = rel_bias[rel_bucket(k_pos[sel] - q_pos[:, None])].astype(jnp.float32)
        logits = logits + jnp.transpose(bias, (0, 2, 1))
        logits = jnp.where(valid[:, None, :], logits, NEG_INF)
        p = jax.nn.softmax(logits, axis=-1).astype(v_sel.dtype)
        return jnp.einsum('qhj,qjhd->qhd', p, v_sel)

    out = lax.map(one_block, (blocks(q), blocks(qi), blocks(wi), b_idx, q_start))
    return out.reshape(bsz, t_len, heads * dh)


def even_mixer(h, shift_buf, wkv0, gla0, w_in, mu, w0, w_decay_up, a0, w_iclr_up, w_gate_up,
               k_k, k_a, r_k, gn_a_g, gn_a_b, alpha_up, alpha_b, gn_b_g, w_out):
    bsz, t_len, _ = h.shape
    f32 = jnp.float32
    p = h @ w_in
    pa, new_shift = token_shift(p[..., :A_COLS], shift_buf, mu)
    pa = pa.astype(f32)
    o3 = 3 * A_WIDTH
    r, k, v, xw, xa, xg = jnp.split(pa, [A_WIDTH, 2 * A_WIDTH, o3, o3 + DECAY_LORA, o3 + DECAY_LORA + ICLR_LORA], axis=-1)
    w_log = -jax.nn.softplus(-(w0 + jnp.tanh(xw) @ w_decay_up)) - 0.5
    decay = jnp.exp(-jnp.exp(w_log))
    a = jax.nn.sigmoid(a0 + xa @ w_iclr_up)
    g = jax.nn.sigmoid(xg) @ w_gate_up
    kk = k * k_k
    k = k * (1.0 + (a - 1.0) * k_a)

    def heads_a(z):
        return z.reshape(bsz, t_len, A_HEADS, A_HEAD_DIM)

    r, decay, k, v, kk, a = (heads_a(z) for z in (r, decay, k, v, kk, a))
    kk = kk * lax.rsqrt(jnp.maximum(jnp.sum(kk * kk, axis=-1, keepdims=True), 1e-12))
    ya, wkv_new = rwkv7_scan(r, decay, k, v, kk, a, wkv0.astype(f32))
    bonus = (jnp.sum(r * k * r_k, axis=-1, keepdims=True) * v).reshape(bsz, t_len, A_WIDTH)
    out_a = (head_groupnorm(ya, gn_a_g, gn_a_b) + bonus) * g
    pb = p[..., A_COLS:].astype(f32)
    q2 = 2 * B_KEY_WIDTH
    qb, kb, vb, xal, rb = jnp.split(pb, [B_KEY_WIDTH, q2, q2 + B_WIDTH, q2 + B_WIDTH + GLA_RANK], axis=-1)
    log_alpha = jax.nn.log_sigmoid(xal @ alpha_up + alpha_b) / GLA_TAU
    ob, gla_new = gla_chunked(qb.reshape(bsz, t_len, B_HEADS, B_KEY_DIM) * B_KEY_DIM ** -0.5,
                              kb.reshape(bsz, t_len, B_HEADS, B_KEY_DIM),
                              vb.reshape(bsz, t_len, B_HEADS, B_VAL_DIM),
                              log_alpha.reshape(bsz, t_len, B_HEADS, B_KEY_DIM),
                              gla0.astype(f32))
    out_b = head_rmsnorm(ob, gn_b_g) * jax.nn.silu(rb)
    out = jnp.concatenate([out_a, out_b], axis=-1).astype(h.dtype) @ w_out
    return out, new_shift, wkv_new, gla_new


def odd_mixer(h, k_cache, v_cache, ki_cache, w_in, rel_bias, w_out):
    bsz, t_len, _ = h.shape
    o3 = 3 * C_WIDTH
    o4 = o3 + IDX_HEADS * IDX_DIM
    q, k, v, qi, ki, wi = jnp.split(h @ w_in, [C_WIDTH, 2 * C_WIDTH, o3, o4, o4 + IDX_DIM], axis=-1)
    q = q.reshape(bsz, t_len, C_HEADS, C_HEAD_DIM)
    k = k.reshape(bsz, t_len, C_HEADS, C_HEAD_DIM)
    v = v.reshape(bsz, t_len, C_HEADS, C_HEAD_DIM)
    qi = qi.reshape(bsz, t_len, IDX_HEADS, IDX_DIM)
    k_all = jnp.concatenate([k_cache.astype(k.dtype), k], axis=1)
    v_all = jnp.concatenate([v_cache.astype(v.dtype), v], axis=1)
    ki_all = jnp.concatenate([ki_cache.astype(ki.dtype), ki], axis=1)
    out = dsa_attention(q, qi, wi, k_all, v_all, ki_all, rel_bias, k_cache.shape[1])
    return out @ w_out, k, v, ki


def conv_ffn(h, buf, w_up, conv_w, conv_b, w_down):
    t_len = h.shape[1]
    u = h @ w_up
    u_pad = jnp.concatenate([buf.astype(u.dtype), u], axis=1)
    uc = conv_b + conv_w[0] * u_pad[:, 0:t_len]
    for j in range(1, CONV_W):
        uc = uc + conv_w[j] * u_pad[:, j:j + t_len]
    gate, val = jnp.split(uc, 2, axis=-1)
    return (jax.nn.silu(gate) * val) @ w_down, u_pad[:, t_len:]


def run_group(x, c, shift_bufs, wkv_states, gla_states, k_caches, v_caches, ki_caches, ffn_bufs, prm):
    mod = jnp.einsum('bd,ldm->lbm', jax.nn.silu(c), prm['w_ada']) + prm['b_ada'][:, None, :]
    shifts, wkvs, glas, ks, vs, kis, ffns = [], [], [], [], [], [], []
    for i in range(DEPTH):
        sh_m, sc_m, gt_m, sh_f, sc_f, gt_f = jnp.split(mod[i], N_MOD, axis=-1)
        h = modulate(x, prm['norm_mix'][i], sh_m, sc_m)
        j = i // 2
        if i % 2 == 0:
            mix, s_new, wkv_new, gla_new = even_mixer(
                h, shift_bufs[j], wkv_states[j], gla_states[j], prm['w_in_ab'][j], prm['mu_a'][j],
                prm['w0'][j], prm['w_decay_up'][j], prm['a0'][j], prm['w_iclr_up'][j], prm['w_gate_up'][j],
                prm['k_k'][j], prm['k_a'][j], prm['r_k'][j], prm['gn_a_g'][j], prm['gn_a_b'][j],
                prm['alpha_up'][j], prm['alpha_b'][j], prm['gn_b_g'][j], prm['w_out_ab'][j])
            shifts.append(s_new.astype(x.dtype))
            wkvs.append(wkv_new.astype(x.dtype))
            glas.append(gla_new.astype(x.dtype))
        else:
            mix, k_new, v_new, ki_new = odd_mixer(
                h, k_caches[j], v_caches[j], ki_caches[j], prm['w_in_c'][j], prm['rel_bias'], prm['w_out_c'][j])
            ks.append(k_new)
            vs.append(v_new)
            kis.append(ki_new)
        x = x + gt_m[:, None, :] * mix
        h = modulate(x, prm['norm_ffn'][i], sh_f, sc_f)
        f, f_buf = conv_ffn(h, ffn_bufs[i], prm['w_ffn_up'][i], prm['ffn_conv_w'][i], prm['ffn_conv_b'][i], prm['w_ffn_down'][i])
        ffns.append(f_buf)
        x = x + gt_f[:, None, :] * f
    y = rmsnorm(x, prm['norm_final'])
    return (y, jnp.stack(shifts), jnp.stack(wkvs), jnp.stack(glas), jnp.stack(ks), jnp.stack(vs),
            jnp.stack(kis), jnp.stack(ffns))


def setup_inputs(seed: int = 0) -> dict:
    key = jax.random.key(seed)
    ks = iter(jax.random.split(key, 48))

    def nrm(shape, s=1.0):
        return jax.random.normal(next(ks), shape, jnp.float32) * s

    d = D_MODEL
    return {
        'x_prompt': nrm((BATCH, SEQ, d)),
        'x_sample': nrm((DEC_BATCH, DEC_SEQ, d)),
        'c_prompt': nrm((BATCH, d)),
        'c_sample': nrm((DEC_BATCH, d)),
        'state_shift_ab': nrm((N_EVEN, DEC_BATCH, 1, A_COLS)),
        'state_wkv': nrm((N_EVEN, DEC_BATCH, A_HEADS, A_HEAD_DIM, A_HEAD_DIM), 0.3),
        'state_gla': nrm((N_EVEN, DEC_BATCH, B_HEADS, B_KEY_DIM, B_VAL_DIM), 0.3),
        'cache_k': nrm((N_ODD, DEC_BATCH, PAST_LEN, C_HEADS, C_HEAD_DIM)),
        'cache_v': nrm((N_ODD, DEC_BATCH, PAST_LEN, C_HEADS, C_HEAD_DIM)),
        'cache_kidx': nrm((N_ODD, DEC_BATCH, PAST_LEN, IDX_DIM)),
        'state_ffn_conv': nrm((DEPTH, DEC_BATCH, CONV_W - 1, 2 * D_FF)),
        'w_ada': nrm((DEPTH, d, N_MOD * d), d ** -0.5),
        'b_ada': nrm((DEPTH, N_MOD * d), 0.02),
        'norm_mix': 1.0 + nrm((DEPTH, d), 0.02),
        'norm_ffn': 1.0 + nrm((DEPTH, d), 0.02),
        'norm_final': 1.0 + nrm((d,), 0.02),
        'w_in_ab': nrm((N_EVEN, d, AB_COLS), d ** -0.5),
        'mu_a': jax.random.uniform(next(ks), (N_EVEN, A_COLS), jnp.float32),
        'w0': nrm((N_EVEN, A_WIDTH), 0.5),
        'w_decay_up': nrm((N_EVEN, DECAY_LORA, A_WIDTH), 0.5 * DECAY_LORA ** -0.5),
        'a0': nrm((N_EVEN, A_WIDTH), 0.1),
        'w_iclr_up': nrm((N_EVEN, ICLR_LORA, A_WIDTH), ICLR_LORA ** -0.5),
        'w_gate_up': nrm((N_EVEN, GATE_LORA, A_WIDTH), GATE_LORA ** -0.5),
        'k_k': 1.0 + nrm((N_EVEN, A_WIDTH), 0.1),
        'k_a': 1.0 + nrm((N_EVEN, A_WIDTH), 0.1),
        'r_k': nrm((N_EVEN, A_HEADS, A_HEAD_DIM), 0.1),
        'gn_a_g': 1.0 + nrm((N_EVEN, A_WIDTH), 0.02),
        'gn_a_b': nrm((N_EVEN, A_WIDTH), 0.02),
        'alpha_up': nrm((N_EVEN, GLA_RANK, B_KEY_WIDTH), GLA_RANK ** -0.5),
        'alpha_b': nrm((N_EVEN, B_KEY_WIDTH), 0.1),
        'gn_b_g': 1.0 + nrm((N_EVEN, B_WIDTH), 0.02),
        'w_out_ab': nrm((N_EVEN, A_WIDTH + B_WIDTH, d), (A_WIDTH + B_WIDTH) ** -0.5),
        'w_in_c': nrm((N_ODD, d, C_COLS), d ** -0.5),
        'rel_bias': nrm((REL_BUCKETS, C_HEADS), 0.5),
        'w_out_c': nrm((N_ODD, C_WIDTH, d), C_WIDTH ** -0.5),
        'w_ffn_up': nrm((DEPTH, d, 2 * D_FF), d ** -0.5),
        'ffn_conv_w': nrm((DEPTH, CONV_W, 2 * D_FF), CONV_W ** -0.5),
        'ffn_conv_b': nrm((DEPTH, 2 * D_FF), 0.02),
        'w_ffn_down': nrm((DEPTH, D_FF, d), D_FF ** -0.5),
    }


def reference(x_prompt, x_sample, c_prompt, c_sample, state_shift_ab, state_wkv, state_gla,
              cache_k, cache_v, cache_kidx, state_ffn_conv, w_ada, b_ada, norm_mix, norm_ffn, norm_final,
              w_in_ab, mu_a, w0, w_decay_up, a0, w_iclr_up, w_gate_up, k_k, k_a, r_k, gn_a_g, gn_a_b,
              alpha_up, alpha_b, gn_b_g, w_out_ab, w_in_c, rel_bias, w_out_c, w_ffn_up, ffn_conv_w,
              ffn_conv_b, w_ffn_down):
    prm = {
        'w_ada': w_ada, 'b_ada': b_ada, 'norm_mix': norm_mix, 'norm_ffn': norm_ffn, 'norm_final': norm_final,
        'w_in_ab': w_in_ab, 'mu_a': mu_a, 'w0': w0, 'w_decay_up': w_decay_up, 'a0': a0,
        'w_iclr_up': w_iclr_up, 'w_gate_up': w_gate_up, 'k_k': k_k, 'k_a': k_a, 'r_k': r_k,
        'gn_a_g': gn_a_g, 'gn_a_b': gn_a_b, 'alpha_up': alpha_up, 'alpha_b': alpha_b, 'gn_b_g': gn_b_g,
        'w_out_ab': w_out_ab, 'w_in_c': w_in_c, 'rel_bias': rel_bias, 'w_out_c': w_out_c,
        'w_ffn_up': w_ffn_up, 'ffn_conv_w': ffn_conv_w, 'ffn_conv_b': ffn_conv_b, 'w_ffn_down': w_ffn_down,
    }
    dt = x_prompt.dtype
    bp = x_prompt.shape[0]
    y_prompt, shift_p, wkv_p, gla_p, k_p, v_p, kidx_p, ffn_p = run_group(
        x_prompt, c_prompt,
        jnp.zeros((N_EVEN, bp, 1, A_COLS), dt),
        jnp.zeros((N_EVEN, bp, A_HEADS, A_HEAD_DIM, A_HEAD_DIM), dt),
        jnp.zeros((N_EVEN, bp, B_HEADS, B_KEY_DIM, B_VAL_DIM), dt),
        jnp.zeros((N_ODD, bp, 0, C_HEADS, C_HEAD_DIM), dt),
        jnp.zeros((N_ODD, bp, 0, C_HEADS, C_HEAD_DIM), dt),
        jnp.zeros((N_ODD, bp, 0, IDX_DIM), dt),
        jnp.zeros((DEPTH, bp, CONV_W - 1, 2 * D_FF), dt),
        prm)
    y_sample, shift_s, wkv_s, gla_s, k_s, v_s, kidx_s, ffn_s = run_group(
        x_sample, c_sample, state_shift_ab, state_wkv, state_gla, cache_k, cache_v, cache_kidx,
        state_ffn_conv, prm)
    return (y_prompt, y_sample, shift_p, shift_s, wkv_p, wkv_s, gla_p, gla_s, k_p, k_s, v_p, v_s,
            kidx_p, kidx_s, ffn_p, ffn_s)
```

```python
import functools
import math

import jax
import jax.numpy as jnp
import numpy as np
from jax import lax
from jax.experimental import pallas as pl
from jax.experimental.pallas import tpu as pltpu

F32 = jnp.float32
BF16 = jnp.bfloat16
HI = lax.Precision.HIGHEST

D_MODEL = 1024
DEPTH = 4
CHUNK = 64
A_HEADS = 8
A_HEAD_DIM = 64
A_WIDTH = A_HEADS * A_HEAD_DIM
DECAY_LORA = 32
ICLR_LORA = 32
GATE_LORA = 96
LORA_W = DECAY_LORA + ICLR_LORA + GATE_LORA
A_COLS = 3 * A_WIDTH + LORA_W
B_HEADS = 4
B_KEY_DIM = 64
B_VAL_DIM = 128
B_KEY_WIDTH = B_HEADS * B_KEY_DIM
B_WIDTH = B_HEADS * B_VAL_DIM
GLA_RANK = 16
GLA_TAU = 16.0
C_HEADS = 16
C_HEAD_DIM = 64
C_WIDTH = C_HEADS * C_HEAD_DIM
IDX_HEADS = 8
IDX_DIM = 64
TOPK_MAX = 256
QUERY_BLOCK = 128
REL_BUCKETS = 32
REL_MAX_DIST = 256
D_FF = 2816
CONV_W = 3
N_MOD = 6
RMS_EPS = 1e-6
GN_EPS = 64e-5
NEG_INF = -1e30

LANES = 128
SUBLANES = 8
VMEM_LIMIT = 56 * 1024 * 1024

LORA_PAD = 256
XAL_PAD = 128
KEY_TILE = 512
FFN_CHUNK = 256
RWKV_TT = 32


def _cparams(sem):
    return pltpu.CompilerParams(dimension_semantics=sem, vmem_limit_bytes=VMEM_LIMIT)


def _const_spec(shape):
    nd = len(shape)
    return pl.BlockSpec(shape, lambda *_: (0,) * nd, pipeline_mode=pl.Buffered(1))


def _sigmoid(x):
    return 1.0 / (1.0 + jnp.exp(-x))


def _silu(x):
    return x * _sigmoid(x)


def _softplus(x):
    return jnp.maximum(x, 0.0) + jnp.log(1.0 + jnp.exp(-jnp.abs(x)))


def _dot(a, b, precision=None):
    return jnp.dot(a, b, preferred_element_type=F32, precision=precision)


def _dot_nt(a, b):
    return lax.dot_general(a, b, (((1,), (1,)), ((), ())), preferred_element_type=F32)


def _seg_sum(x, ind_bf16):
    hi = x.astype(BF16)
    lo = (x - hi.astype(F32)).astype(BF16)
    return _dot(hi, ind_bf16) + _dot(lo, ind_bf16)


def _lane_half(pair, lane_hi, upper):
    zero = jnp.zeros_like(pair)
    return jnp.where(lane_hi, pair, zero) if upper else jnp.where(lane_hi, zero, pair)


def _modulated_norm(x, gain, shift, scale):
    var = jnp.mean(x * x, axis=-1, keepdims=True)
    h = x * lax.rsqrt(var + RMS_EPS) * gain
    return h * (1.0 + scale) + shift


def _ada_kernel(c_ref, w_ref, b_ref, o_ref):
    c = c_ref[...]
    o_ref[0] = _dot(_silu(c), w_ref[0], HI) + b_ref[0]


def _ada(c, w_ada, b_ada):
    m, d = c.shape
    depth, _, n = w_ada.shape
    tn = 1536
    return pl.pallas_call(
        _ada_kernel,
        grid=(depth, n // tn),
        in_specs=[pl.BlockSpec((m, d), lambda l, j: (0, 0)),
                  pl.BlockSpec((1, d, tn), lambda l, j: (l, 0, j)),
                  pl.BlockSpec((1, 1, tn), lambda l, j: (l, 0, j))],
        out_specs=pl.BlockSpec((1, m, tn), lambda l, j: (l, 0, j)),
        out_shape=jax.ShapeDtypeStruct((depth, m, n), F32),
        compiler_params=_cparams(("arbitrary", "arbitrary")),
        name="ada_mod",
    )(c, w_ada, b_ada.reshape(depth, 1, n))


def _proj_kernel(x_ref, gain_ref, shift_ref, scale_ref, w_ref, *out_refs, groups):
    h = _modulated_norm(x_ref[0], gain_ref[...], shift_ref[0], scale_ref[0]).astype(BF16)
    refs = iter(out_refs)
    for c0, n, dtypes in groups:
        y = _dot(h, w_ref[:, c0:c0 + n])
        for dt in dtypes:
            next(refs)[0] = y.astype(dt)


def _proj(x, gain, shift, scale, w, groups, tm):
    b, t, d = x.shape
    n_all = w.shape[1]
    mod_spec = pl.BlockSpec((1, 1, d), lambda i, j: (i, 0, 0))
    outs = [(n, dt) for _, n, dts in groups for dt in dts]
    return pl.pallas_call(
        functools.partial(_proj_kernel, groups=groups),
        grid=(b, t // tm),
        in_specs=[pl.BlockSpec((1, tm, d), lambda i, j: (i, j, 0)),
                  _const_spec((1, d)), mod_spec, mod_spec,
                  _const_spec((d, n_all))],
        out_specs=[pl.BlockSpec((1, tm, n), lambda i, j: (i, j, 0)) for n, _ in outs],
        out_shape=[jax.ShapeDtypeStruct((b, t, n), dt) for n, dt in outs],
        compiler_params=_cparams(("arbitrary", "arbitrary")),
        name="norm_proj",
    )(x, gain, shift, scale, w)


def _ffn_kernel(x_ref, gain_ref, shift_ref, scale_ref, gate_ref, buf_ref, wup_ref, cw_ref, cb_ref, wdn_ref,
                fgain_ref, xo_ref, bufo_ref, carry_ref, *, tm, final):
    t = pl.program_id(1)

    @pl.when(t == 0)
    def _():
        carry_ref[...] = buf_ref[0]

    x = x_ref[0]
    h = _modulated_norm(x, gain_ref[...], shift_ref[0], scale_ref[0]).astype(BF16)
    row = lax.broadcasted_iota(jnp.int32, (tm, FFN_CHUNK), 0)

    def conv(u, c0):
        c_old = carry_ref[0:1, c0:c0 + FFN_CHUNK]
        c_new = carry_ref[1:2, c0:c0 + FFN_CHUNK]
        prev1 = jnp.where(row == 0, c_new, pltpu.roll(u, 1, 0))
        prev2 = jnp.where(row == 0, c_old, jnp.where(row == 1, c_new, pltpu.roll(u, 2, 0)))
        carry_ref[:, c0:c0 + FFN_CHUNK] = u[tm - 2:tm, :]
        return (cb_ref[:, c0:c0 + FFN_CHUNK] + cw_ref[0:1, c0:c0 + FFN_CHUNK] * prev2
                + cw_ref[1:2, c0:c0 + FFN_CHUNK] * prev1 + cw_ref[2:3, c0:c0 + FFN_CHUNK] * u)

    acc = jnp.zeros((tm, D_MODEL), F32)
    for c in range(D_FF // FFN_CHUNK):
        g0 = c * FFN_CHUNK
        v0 = D_FF + g0
        gate = conv(_dot(h, wup_ref[:, g0:g0 + FFN_CHUNK]), g0)
        val = conv(_dot(h, wup_ref[:, v0:v0 + FFN_CHUNK]), v0)
        act = (_silu(gate) * val).astype(BF16)
        acc = acc + _dot(act, wdn_ref[g0:g0 + FFN_CHUNK, :])

    y = x + gate_ref[0] * acc
    if final:
        y = y * lax.rsqrt(jnp.mean(y * y, axis=-1, keepdims=True) + RMS_EPS) * fgain_ref[...]
    xo_ref[0] = y

    @pl.when(t == pl.num_programs(1) - 1)
    def _():
        bufo_ref[0] = carry_ref[...]


def _ffn(x, gain, shift, scale, gate, buf, w_up, conv_w, conv_b, w_down, final_gain, tm, final):
    b, t, d = x.shape
    f2 = 2 * D_FF
    mod_spec = pl.BlockSpec((1, 1, d), lambda i, j: (i, 0, 0))
    buf_spec = pl.BlockSpec((1, CONV_W - 1, f2), lambda i, j: (i, 0, 0))
    return pl.pallas_call(
        functools.partial(_ffn_kernel, tm=tm, final=final),
        grid=(b, t // tm),
        in_specs=[pl.BlockSpec((1, tm, d), lambda i, j: (i, j, 0)),
                  _const_spec((1, d)), mod_spec, mod_spec, mod_spec, buf_spec,
                  _const_spec((d, f2)), _const_spec((CONV_W, f2)), _const_spec((1, f2)),
                  _const_spec((D_FF, d)), _const_spec((1, d))],
        out_specs=[pl.BlockSpec((1, tm, d), lambda i, j: (i, j, 0)), buf_spec],
        out_shape=[jax.ShapeDtypeStruct((b, t, d), F32),
                   jax.ShapeDtypeStruct((b, CONV_W - 1, f2), F32)],
        scratch_shapes=[pltpu.VMEM((CONV_W - 1, f2), F32)],
        compiler_params=_cparams(("arbitrary", "arbitrary")),
        name="conv_ffn",
    )(x, gain, shift, scale, gate, buf, w_up, conv_w, conv_b, w_down, final_gain)


def _pre_even_kernel(rkv_ref, lx_ref, sb_rkv_ref, sb_l_ref, mu_rkv_ref, mu_l_ref, w0_ref, wd_ref, a0_ref, wi_ref,
                     wg_ref, kk_ref, ka_ref, rk_ref, ind_ref, x6_ref, g_ref, bonus_ref, c_rkv, c_l, *, tm):
    t = pl.program_id(1)

    @pl.when(t == 0)
    def _():
        c_rkv[...] = sb_rkv_ref[0]
        c_l[...] = sb_l_ref[0]

    def shifted(p, carry_ref, mu):
        row = lax.broadcasted_iota(jnp.int32, p.shape, 0)
        prev = jnp.where(row == 0, carry_ref[...], pltpu.roll(p, 1, 0))
        carry_ref[...] = p[tm - 1:tm, :]
        return p + (prev - p) * mu

    pa = shifted(rkv_ref[0], c_rkv, mu_rkv_ref[...])
    lora = shifted(lx_ref[0][:, :LORA_PAD], c_l, mu_l_ref[...])
    r = pa[:, :A_WIDTH]
    k = pa[:, A_WIDTH:2 * A_WIDTH]
    v = pa[:, 2 * A_WIDTH:]
    w_log = -_softplus(-(w0_ref[...] + _dot(jnp.tanh(lora), wd_ref[...], HI))) - 0.5
    decay = jnp.exp(-jnp.exp(w_log))
    a = _sigmoid(a0_ref[...] + _dot(lora, wi_ref[...], HI))
    g = _dot(_sigmoid(lora), wg_ref[...], HI)
    kk = k * kk_ref[...]
    k2 = k * (1.0 + (a - 1.0) * ka_ref[...])
    ind = ind_ref[...]
    kk = kk * lax.rsqrt(jnp.maximum(_seg_sum(kk * kk, ind), 1e-12))
    bonus = _seg_sum(r * k2 * rk_ref[...], ind) * v
    x6_ref[0, 0] = r
    x6_ref[1, 0] = decay
    x6_ref[2, 0] = k2
    x6_ref[3, 0] = v
    x6_ref[4, 0] = kk
    x6_ref[5, 0] = kk * a
    g_ref[0] = g
    bonus_ref[0] = bonus


def _pre_even(rkv, lx, sb_rkv, sb_l, prm, tm):
    b, t, _ = rkv.shape
    aw = A_WIDTH
    row = lambda n: _const_spec((1, n))
    return pl.pallas_call(
        functools.partial(_pre_even_kernel, tm=tm),
        grid=(b, t // tm),
        in_specs=[pl.BlockSpec((1, tm, 3 * aw), lambda i, j: (i, j, 0)),
                  pl.BlockSpec((1, tm, LORA_PAD + XAL_PAD), lambda i, j: (i, j, 0)),
                  pl.BlockSpec((1, 1, 3 * aw), lambda i, j: (i, 0, 0)),
                  pl.BlockSpec((1, 1, LORA_PAD), lambda i, j: (i, 0, 0)),
                  row(3 * aw), row(LORA_PAD), row(aw), _const_spec((LORA_PAD, aw)), row(aw),
                  _const_spec((LORA_PAD, aw)), _const_spec((LORA_PAD, aw)), row(aw), row(aw), row(aw),
                  _const_spec((aw, aw))],
        out_specs=[pl.BlockSpec((6, 1, tm, aw), lambda i, j: (0, i, j, 0)),
                   pl.BlockSpec((1, tm, aw), lambda i, j: (i, j, 0)),
                   pl.BlockSpec((1, tm, aw), lambda i, j: (i, j, 0))],
        out_shape=[jax.ShapeDtypeStruct((6, b, t, aw), F32),
                   jax.ShapeDtypeStruct((b, t, aw), F32),
                   jax.ShapeDtypeStruct((b, t, aw), F32)],
        scratch_shapes=[pltpu.VMEM((1, 3 * aw), F32), pltpu.VMEM((1, LORA_PAD), F32)],
        compiler_params=_cparams(("arbitrary", "arbitrary")),
        name="rwkv_prep",
    )(rkv, lx, sb_rkv, sb_l, prm['mu_rkv'], prm['mu_l'], prm['w0'], prm['wd'], prm['a0'], prm['wi'], prm['wg'],
      prm['k_k'], prm['k_a'], prm['r_k'], prm['ind64'])


def _rwkv_kernel(x_ref, s0_ref, y_ref, so_ref, s_scr, *, tt):
    n = A_HEAD_DIM
    g = n // SUBLANES
    i = pl.program_id(0)

    @pl.when(i == 0)
    def _():
        s_scr[...] = s0_ref[...]

    def allsum(x):
        x = x + pltpu.roll(x, 4, 0)
        x = x + pltpu.roll(x, 2, 0)
        return x + pltpu.roll(x, 1, 0)

    def ksum(x3):
        return allsum(jnp.sum(x3, axis=0))

    def step(t, carry):
        def ld(j):
            return x_ref[t, j].reshape(g, SUBLANES, LANES)
        r, w, k, kk, b = ld(0), ld(1), ld(2), ld(4), ld(5)
        wr = w * r
        br = ksum(b * r)
        kr = ksum(k * r)

        def vbody(v, c):
            sv = s_scr[v].reshape(g, SUBLANES, LANES)
            sa = ksum(sv * kk)
            z = ksum(sv * wr)
            vv = jnp.broadcast_to(x_ref[t, 3, pl.ds(v, 1), :], (SUBLANES, LANES))
            s_scr[v] = (sv * w - sa[None] * b + vv[None] * k).reshape(n, LANES)
            y = z - sa * br + vv * kr
            y_ref[t, pl.ds(v, 1), :] = y[0:1]
            return c

        lax.fori_loop(0, n, vbody, 0, unroll=4)
        return carry

    lax.fori_loop(0, tt, step, 0)

    @pl.when(i == pl.num_programs(0) - 1)
    def _():
        so_ref[...] = s_scr[...]


def _rwkv(x6t, s0, tt):
    t = x6t.shape[0]
    n = A_HEAD_DIM
    return pl.pallas_call(
        functools.partial(_rwkv_kernel, tt=tt),
        grid=(t // tt,),
        in_specs=[pl.BlockSpec((tt, 6, n, LANES), lambda i: (i, 0, 0, 0)),
                  pl.BlockSpec((n, n, LANES), lambda i: (0, 0, 0))],
        out_specs=[pl.BlockSpec((tt, n, LANES), lambda i: (i, 0, 0)),
                   pl.BlockSpec((n, n, LANES), lambda i: (0, 0, 0))],
        out_shape=[jax.ShapeDtypeStruct((t, n, LANES), F32),
                   jax.ShapeDtypeStruct((n, n, LANES), F32)],
        scratch_shapes=[pltpu.VMEM((n, n, LANES), F32)],
        compiler_params=_cparams(("arbitrary",)),
        name="rwkv_scan",
    )(x6t, s0)


def _gla_tables(c):
    idx = np.arange(c)
    t = idx[:, None]
    r = idx[None, :]
    mats = [(r <= t), (r > t)]
    masks = [np.eye(c, dtype=bool)]
    m = c // 2
    while m >= 1:
        mid = (idx // (2 * m)) * 2 * m + m - 1
        right = (idx % (2 * m)) >= m
        mats.append(right[:, None] & (r > mid[:, None]) & (r <= t))
        mats.append((~right)[:, None] & (r > t) & (r <= mid[:, None]))
        masks.append(((idx // (2 * m))[:, None] == (idx // (2 * m))[None, :]) & right[:, None] & (~right)[None, :])
        m //= 2
    return (np.concatenate(mats, 0).astype(np.float32), np.stack(masks).astype(np.float32))


def _gla_kernel(qk_ref, v_ref, lx_ref, au_ref, ab_ref, mall_ref, masks_ref, s0_ref, o_ref, so_ref, s_scr, *, c, nlev):
    i = pl.program_id(1)

    @pl.when(i == 0)
    def _():
        s_scr[...] = s0_ref[0]

    z = _dot(lx_ref[0], au_ref[...], HI) + ab_ref[...]
    g = -_softplus(-z) * (1.0 / GLA_TAU)
    d_all = _dot(mall_ref[...], g, HI)
    qk = qk_ref[0]
    q = qk[:, :B_KEY_WIDTH] * (B_KEY_DIM ** -0.5)
    k = qk[:, B_KEY_WIDTH:]
    v = v_ref[0]
    b_inc = d_all[0:c]
    b_rev = d_all[c:2 * c]
    q_in = (q * jnp.exp(b_inc)).astype(BF16)
    k_st = (k * jnp.exp(b_rev)).astype(BF16)
    dec_last = jnp.exp(b_inc[c - 1:c, :])
    qd = [q.astype(BF16)]
    kd = [k.astype(BF16)]
    for lev in range(nlev):
        dq = d_all[(2 + 2 * lev) * c:(3 + 2 * lev) * c]
        dk = d_all[(3 + 2 * lev) * c:(4 + 2 * lev) * c]
        qd.append((q * jnp.exp(dq)).astype(BF16))
        kd.append((k * jnp.exp(dk)).astype(BF16))
    for h in range(B_HEADS):
        ks = slice(h * B_KEY_DIM, (h + 1) * B_KEY_DIM)
        vs = slice(h * B_VAL_DIM, (h + 1) * B_VAL_DIM)
        att = jnp.zeros((c, c), F32)
        for lev in range(nlev + 1):
            att = att + masks_ref[lev] * _dot_nt(qd[lev][:, ks], kd[lev][:, ks])
        vh = v[:, vs]
        st = s_scr[h]
        o_ref[0, :, vs] = _dot_nt(q_in[:, ks], st.astype(BF16)) + _dot(att.astype(BF16), vh.astype(BF16))
        s_scr[h] = st * dec_last[:, ks] + _dot(vh.T.astype(BF16), k_st[:, ks])

    @pl.when(i == pl.num_programs(1) - 1)
    def _():
        so_ref[0] = s_scr[...]


def _gla(qk, v, lx, alpha_up, alpha_b, s0t):
    b, t, _ = qk.shape
    c = min(CHUNK, t)
    nlev = int(round(math.log2(c)))
    mall, masks = _gla_tables(c)
    kw, vw = B_KEY_WIDTH, B_WIDTH
    st_spec = pl.BlockSpec((1, B_HEADS, B_VAL_DIM, B_KEY_DIM), lambda i, j: (i, 0, 0, 0))
    return pl.pallas_call(
        functools.partial(_gla_kernel, c=c, nlev=nlev),
        grid=(b, t // c),
        in_specs=[pl.BlockSpec((1, c, 2 * kw), lambda i, j: (i, j, 0)),
                  pl.BlockSpec((1, c, vw), lambda i, j: (i, j, 0)),
                  pl.BlockSpec((1, c, XAL_PAD), lambda i, j: (i, j, LORA_PAD // XAL_PAD)),
                  _const_spec((XAL_PAD, kw)), _const_spec((1, kw)),
                  _const_spec(mall.shape), _const_spec(masks.shape), st_spec],
        out_specs=[pl.BlockSpec((1, c, vw), lambda i, j: (i, j, 0)), st_spec],
        out_shape=[jax.ShapeDtypeStruct((b, t, vw), F32),
                   jax.ShapeDtypeStruct((b, B_HEADS, B_VAL_DIM, B_KEY_DIM), F32)],
        scratch_shapes=[pltpu.VMEM((B_HEADS, B_VAL_DIM, B_KEY_DIM), F32)],
        compiler_params=_cparams(("arbitrary", "arbitrary")),
        name="gla_chunk",
    )(qk, v, lx, alpha_up, alpha_b, jnp.asarray(mall), jnp.asarray(masks), s0t)


def _post_even_kernel(x_ref, gate_ref, ya_ref, bonus_ref, g_ref, ob_ref, rb_ref, gag_ref, gab_ref, gbg_ref,
                      ind64_ref, ind128_ref, wout_ref, xo_ref):
    ya = ya_ref[0]
    ind64 = ind64_ref[...]
    mu = _seg_sum(ya, ind64) * (1.0 / A_HEAD_DIM)
    dev = ya - mu
    var = _seg_sum(dev * dev, ind64) * (1.0 / A_HEAD_DIM)
    out_a = (dev * lax.rsqrt(var + GN_EPS) * gag_ref[...] + gab_ref[...] + bonus_ref[0]) * g_ref[0]
    ob = ob_ref[0]
    ms = _seg_sum(ob * ob, ind128_ref[...]) * (1.0 / B_VAL_DIM)
    out_b = ob * lax.rsqrt(ms + RMS_EPS) * gbg_ref[...] * _silu(rb_ref[0])
    mix = _dot(out_a.astype(BF16), wout_ref[:A_WIDTH, :]) + _dot(out_b.astype(BF16), wout_ref[A_WIDTH:, :])
    xo_ref[0] = x_ref[0] + gate_ref[0] * mix


def _post_even(x, gate, ya, bonus, g, ob, rb, prm, tm):
    b, t, d = x.shape
    aw = A_WIDTH
    tok = lambda n: pl.BlockSpec((1, tm, n), lambda i, j: (i, j, 0))
    return pl.pallas_call(
        _post_even_kernel,
        grid=(b, t // tm),
        in_specs=[tok(d), pl.BlockSpec((1, 1, d), lambda i, j: (i, 0, 0)), tok(aw), tok(aw), tok(aw), tok(B_WIDTH),
                  tok(B_WIDTH), _const_spec((1, aw)), _const_spec((1, aw)), _const_spec((1, B_WIDTH)),
                  _const_spec((aw, aw)), _const_spec((B_WIDTH, B_WIDTH)), _const_spec((aw + B_WIDTH, d))],
        out_specs=tok(d),
        out_shape=jax.ShapeDtypeStruct((b, t, d), F32),
        compiler_params=_cparams(("arbitrary", "arbitrary")),
        name="even_out",
    )(x, gate, ya, bonus, g, ob, rb, prm['gn_a_g'], prm['gn_a_b'], prm['gn_b_g'], prm['ind64'], prm['ind128'],
      prm['w_out'])


def _attn_kernel(far_ref, x_ref, gate_ref, q_ref, qi_ref, wi_ref, k_ref, v_ref, ki_ref, near_ref, tri_ref, wout_ref,
                 xo_ref, key_scr, mask_scr, s_scr, o_scr, *, tq, past, l_len, topk):
    tk = KEY_TILE
    q0 = past + pl.program_id(1) * tq
    row_pos = q0 + lax.broadcasted_iota(jnp.int32, (tq, 1), 0)
    lim = jnp.minimum((row_pos // CHUNK + 1) * CHUNK, l_len)
    n_adm = jnp.minimum(((q0 + tq - 1) // CHUNK + 1) * CHUNK, l_len)
    nt = (n_adm + tk - 1) // tk
    lane_hi = lax.broadcasted_iota(jnp.int32, (tq, LANES), 1) >= (LANES // 2)
    key_neg = _order_key_of(NEG_INF)

    w_cols = [wi_ref[0][:, h:h + 1] * (IDX_HEADS ** -0.5 * IDX_DIM ** -0.5) for h in range(IDX_HEADS)]
    qi_heads = []
    for h in range(IDX_HEADS):
        pair = qi_ref[0, :, (h // 2) * LANES:(h // 2 + 1) * LANES]
        qi_heads.append(_lane_half(pair, lane_hi, h % 2 == 1))

    def score_tile(t, c):
        off = pl.multiple_of(t * tk, tk)
        ki = ki_ref[0, pl.ds(off, tk), :]
        score = jnp.zeros((tq, tk), F32)
        for h in range(IDX_HEADS):
            score = score + jnp.maximum(_dot_nt(qi_heads[h], ki), 0.0) * w_cols[h]
        kpos = off + lax.broadcasted_iota(jnp.int32, (tq, tk), 1)
        score = jnp.where(score == 0.0, 0.0, score)
        score = jnp.where(kpos < lim, score, NEG_INF)
        bits = lax.bitcast_convert_type(score, jnp.int32)
        key_scr[:, pl.ds(off, tk)] = bits ^ ((bits >> 31) & 0x7FFFFFFF)
        return c

    lax.fori_loop(0, nt, score_tile, 0)

    def count_ge(thr):
        def body(t, acc):
            off = pl.multiple_of(t * tk, tk)
            hit = jnp.where(key_scr[:, pl.ds(off, tk)] >= thr, 1.0, 0.0)
            for j in range(tk // LANES):
                acc = acc + hit[:, j * LANES:(j + 1) * LANES]
            return acc
        acc = lax.fori_loop(0, nt, body, jnp.zeros((tq, LANES), F32))
        return jnp.sum(acc, axis=1, keepdims=True)

    int_min = jnp.full((tq, 1), -2 ** 31, jnp.int32)
    thr = jnp.where(count_ge(jnp.zeros((tq, 1), jnp.int32)) >= topk, 0, int_min)

    def bit_step(i, thr):
        cand = thr + (jnp.int32(1) << (30 - i))
        return jnp.where(count_ge(cand) >= topk, cand, thr)

    thr = lax.fori_loop(0, 31, bit_step, thr)
    need = topk - count_ge(thr + 1)

    def mask_tile(t, run):
        off = pl.multiple_of(t * tk, tk)
        key = key_scr[:, pl.ds(off, tk)]
        eq = key == thr
        prefix = run + _dot(jnp.where(eq, 1.0, 0.0).astype(BF16), tri_ref[...])
        sel = ((key > thr) | (eq & (prefix <= need))) & (key > key_neg)
        mask_scr[:, pl.ds(off, tk)] = jnp.where(sel, 0.0, NEG_INF)
        return prefix[:, tk - 1:tk]

    lax.fori_loop(0, nt, mask_tile, jnp.zeros((tq, 1), F32))

    jq = q0 // LANES

    def pair_body(p, c):
        pc = pl.multiple_of(p * LANES, LANES)
        qpair = q_ref[0, :, pl.ds(pc, LANES)] * (C_HEAD_DIM ** -0.5)
        outs = []
        for e in range(2):
            h = 2 * p + e
            far = far_ref[h]
            qh = _lane_half(qpair, lane_hi, e == 1)

            def logits_tile(t, cc):
                off = pl.multiple_of(t * tk, tk)
                s_scr[:, pl.ds(off, tk)] = (_dot_nt(qh, k_ref[0, pl.ds(off, tk), pl.ds(pc, LANES)]) + far
                                            + mask_scr[:, pl.ds(off, tk)])
                return cc

            lax.fori_loop(0, nt, logits_tile, 0)
            for nidx in range(3):
                j = jq - 2 + nidx

                @pl.when(j >= 0)
                def _():
                    off = pl.multiple_of(j * LANES, LANES)
                    s_scr[:, pl.ds(off, LANES)] = s_scr[:, pl.ds(off, LANES)] + (near_ref[nidx, h] - far)

            def max_tile(t, m):
                off = pl.multiple_of(t * tk, tk)
                s = s_scr[:, pl.ds(off, tk)]
                for j in range(tk // LANES):
                    m = jnp.maximum(m, s[:, j * LANES:(j + 1) * LANES])
                return m

            m = lax.fori_loop(0, nt, max_tile, jnp.full((tq, LANES), NEG_INF, F32))
            m = jnp.max(m, axis=1, keepdims=True)

            def pv_tile(t, carry):
                l, acc = carry
                off = pl.multiple_of(t * tk, tk)
                pexp = jnp.exp(s_scr[:, pl.ds(off, tk)] - m)
                for j in range(tk // LANES):
                    l = l + pexp[:, j * LANES:(j + 1) * LANES]
                acc = acc + _dot(pexp.astype(BF16), v_ref[0, pl.ds(off, tk), pl.ds(pc, LANES)])
                return l, acc

            l, acc = lax.fori_loop(0, nt, pv_tile, (jnp.zeros((tq, LANES), F32), jnp.zeros((tq, LANES), F32)))
            outs.append(acc / jnp.sum(l, axis=1, keepdims=True))
        o_scr[:, pl.ds(pc, LANES)] = jnp.where(lane_hi, outs[1], outs[0])
        return c

    lax.fori_loop(0, C_HEADS // 2, pair_body, 0)
    mix = _dot(o_scr[...].astype(BF16), wout_ref[...])
    xo_ref[0] = x_ref[0] + gate_ref[0] * mix


def _order_key_of(value):
    bits = int(np.float32(value).view(np.int32))
    return bits ^ ((bits >> 31) & 0x7FFFFFFF)


def _rel_bucket(rel):
    half = REL_BUCKETS // 2
    max_exact = half // 2
    n = jnp.abs(rel)
    nf = jnp.maximum(n, 1).astype(F32)
    large = max_exact + (jnp.log(nf / max_exact) / math.log(REL_MAX_DIST / max_exact)
                         * (half - max_exact)).astype(jnp.int32)
    large = jnp.minimum(large, half - 1)
    return jnp.where(rel > 0, half, 0) + jnp.where(n < max_exact, n, large)


def _attn(x, gate, q, qi, wi, k_all, v_all, ki_all, rel_bias, w_out, past, l_len):
    b, t, d = x.shape
    lp = k_all.shape[1]
    tq = min(QUERY_BLOCK, t)
    topk = min(TOPK_MAX, l_len // 4)
    assert t % tq == 0 and past % LANES == 0 and lp % KEY_TILE == 0 and lp >= l_len
    assert 3 * LANES - (LANES - 1) >= REL_MAX_DIST and tq <= LANES
    rel = ((jnp.arange(3) - 2) * LANES)[:, None, None] + jnp.arange(LANES)[None, None, :] - jnp.arange(tq)[None, :, None]
    near = jnp.transpose(rel_bias[_rel_bucket(rel)], (0, 3, 1, 2)).astype(F32)
    far = rel_bias[_rel_bucket(jnp.int32(-REL_MAX_DIST))].astype(F32)
    tri = jnp.asarray(np.triu(np.ones((KEY_TILE, KEY_TILE), np.float32)), BF16)
    tok = lambda n: pl.BlockSpec((1, tq, n), lambda i, j: (i, j, 0))
    seq = lambda n: pl.BlockSpec((1, lp, n), lambda i, j: (i, 0, 0), pipeline_mode=pl.Buffered(1))
    return pl.pallas_call(
        functools.partial(_attn_kernel, tq=tq, past=past, l_len=l_len, topk=topk),
        grid=(b, t // tq),
        in_specs=[pl.BlockSpec(memory_space=pltpu.SMEM),
                  tok(d), pl.BlockSpec((1, 1, d), lambda i, j: (i, 0, 0)), tok(C_WIDTH), tok(IDX_HEADS * IDX_DIM),
                  tok(LANES), seq(C_WIDTH), seq(C_WIDTH), seq(LANES),
                  _const_spec((3, C_HEADS, tq, LANES)), _const_spec((KEY_TILE, KEY_TILE)),
                  _const_spec((C_WIDTH, d))],
        out_specs=tok(d),
        out_shape=jax.ShapeDtypeStruct((b, t, d), F32),
        scratch_shapes=[pltpu.VMEM((tq, lp), jnp.int32), pltpu.VMEM((tq, lp), F32), pltpu.VMEM((tq, lp), F32),
                        pltpu.VMEM((tq, C_WIDTH), F32)],
        compiler_params=_cparams(("arbitrary", "arbitrary")),
        name="dsa_attn",
    )(far, x, gate, q, qi, wi, k_all, v_all, ki_all, near, tri, w_out)


def _prep_params(p):
    d = D_MODEL
    aw = A_WIDTH
    out = {'even': [], 'odd': [], 'ffn': []}
    ind64 = jnp.asarray(np.kron(np.eye(aw // A_HEAD_DIM), np.ones((A_HEAD_DIM, A_HEAD_DIM))), BF16)
    ind128 = jnp.asarray(np.kron(np.eye(B_WIDTH // B_VAL_DIM), np.ones((B_VAL_DIM, B_VAL_DIM))), BF16)
    for j in range(p['w_in_ab'].shape[0]):
        w = p['w_in_ab'][j]
        o = A_COLS
        lora = jnp.pad(w[:, 3 * aw:A_COLS], ((0, 0), (0, LORA_PAD - LORA_W)))
        xal = jnp.pad(w[:, o + 2 * B_KEY_WIDTH + B_WIDTH:o + 2 * B_KEY_WIDTH + B_WIDTH + GLA_RANK],
                      ((0, 0), (0, XAL_PAD - GLA_RANK)))
        w_in = jnp.concatenate([w[:, :3 * aw], w[:, o:o + 2 * B_KEY_WIDTH + B_WIDTH],
                                w[:, o + 2 * B_KEY_WIDTH + B_WIDTH + GLA_RANK:], lora, xal], axis=1).astype(BF16)
        mu = p['mu_a'][j]

        def lora_rows(m, r0):
            return jnp.pad(m, ((r0, LORA_PAD - r0 - m.shape[0]), (0, 0)))

        out['even'].append({
            'w_in': w_in,
            'mu_rkv': mu[None, :3 * aw],
            'mu_l': jnp.pad(mu[None, 3 * aw:], ((0, 0), (0, LORA_PAD - LORA_W))),
            'w0': p['w0'][j][None], 'a0': p['a0'][j][None],
            'wd': lora_rows(p['w_decay_up'][j], 0),
            'wi': lora_rows(p['w_iclr_up'][j], DECAY_LORA),
            'wg': lora_rows(p['w_gate_up'][j], DECAY_LORA + ICLR_LORA),
            'k_k': p['k_k'][j][None], 'k_a': p['k_a'][j][None], 'r_k': p['r_k'][j].reshape(1, aw),
            'gn_a_g': p['gn_a_g'][j][None], 'gn_a_b': p['gn_a_b'][j][None], 'gn_b_g': p['gn_b_g'][j][None],
            'alpha_up': jnp.pad(p['alpha_up'][j], ((0, XAL_PAD - GLA_RANK), (0, 0))),
            'alpha_b': p['alpha_b'][j][None],
            'w_out': p['w_out_ab'][j].astype(BF16),
            'ind64': ind64, 'ind128': ind128,
        })
    for j in range(p['w_in_c'].shape[0]):
        w = p['w_in_c'][j]
        o4 = 3 * C_WIDTH + IDX_HEADS * IDX_DIM
        ki = w[:, o4:o4 + IDX_DIM]
        wi = jnp.pad(w[:, o4 + IDX_DIM:], ((0, 0), (0, LANES - IDX_HEADS)))
        out['odd'].append({
            'w_in': jnp.concatenate([w[:, :o4], ki, ki, wi], axis=1).astype(BF16),
            'w_out': p['w_out_c'][j].astype(BF16),
        })
    for i in range(DEPTH):
        out['ffn'].append({
            'w_up': p['w_ffn_up'][i].astype(BF16), 'conv_w': p['ffn_conv_w'][i], 'conv_b': p['ffn_conv_b'][i][None],
            'w_down': p['w_ffn_down'][i].astype(BF16),
        })
    return out


def _token_tile(t):
    return min(512, t)


def _even_layer(x, mods, gain, shift_buf, wkv0, gla0, prm):
    b, t, d = x.shape
    aw = A_WIDTH
    tm = _token_tile(t)
    sh_m, sc_m, gt_m = mods
    kw2 = 2 * B_KEY_WIDTH
    cols = [(0, 3 * aw, (F32,)), (3 * aw, kw2, (F32,)), (3 * aw + kw2, B_WIDTH, (F32,)),
            (3 * aw + kw2 + B_WIDTH, B_WIDTH, (F32,)), (3 * aw + kw2 + 2 * B_WIDTH, LORA_PAD + XAL_PAD, (F32,))]
    rkv, qk_b, v_b, r_b, lx = _proj(x, gain, sh_m, sc_m, prm['w_in'], cols, tm)
    new_shift = jnp.concatenate([rkv[:, t - 1:, :], lx[:, t - 1:, :LORA_W]], axis=-1)
    sb_rkv = shift_buf[:, :, :3 * aw]
    sb_l = jnp.pad(shift_buf[:, :, 3 * aw:], ((0, 0), (0, 0), (0, LORA_PAD - LORA_W)))
    x6, g, bonus = _pre_even(rkv, lx, sb_rkv, sb_l, prm, tm)
    chains = b * A_HEADS
    assert chains == LANES
    x6t = jnp.transpose(x6.reshape(6, b, t, A_HEADS, A_HEAD_DIM), (2, 0, 4, 1, 3)).reshape(t, 6, A_HEAD_DIM, chains)
    s0 = jnp.transpose(wkv0, (2, 3, 0, 1)).reshape(A_HEAD_DIM, A_HEAD_DIM, chains)
    ya_t, s_fin = _rwkv(x6t, s0, min(RWKV_TT, t))
    ya = jnp.transpose(ya_t.reshape(t, A_HEAD_DIM, b, A_HEADS), (2, 0, 3, 1)).reshape(b, t, aw)
    wkv_new = jnp.transpose(s_fin.reshape(A_HEAD_DIM, A_HEAD_DIM, b, A_HEADS), (2, 3, 0, 1))
    ob, gla_t = _gla(qk_b, v_b, lx, prm['alpha_up'], prm['alpha_b'], jnp.swapaxes(gla0, 2, 3))
    gla_new = jnp.swapaxes(gla_t, 2, 3)
    x = _post_even(x, gt_m, ya, bonus, g, ob, r_b, prm, tm)
    return x, new_shift, wkv_new, gla_new


def _odd_layer(x, mods, gain, k_cache, v_cache, ki_cache, rel_bias, prm):
    b, t, d = x.shape
    tm = _token_tile(t)
    sh_m, sc_m, gt_m = mods
    cw = C_WIDTH
    iw = IDX_HEADS * IDX_DIM
    cols = [(0, cw, (BF16,)), (cw, cw, (F32, BF16)), (2 * cw, cw, (F32, BF16)), (3 * cw, iw, (BF16,)),
            (3 * cw + iw, LANES, (F32, BF16)), (3 * cw + iw + LANES, LANES, (F32,))]
    q, k, k_bf, v, v_bf, qi, kiki, kiki_bf, wi = _proj(x, gain, sh_m, sc_m, prm['w_in'], cols, tm)
    past = k_cache.shape[1]
    l_len = past + t
    lp = -(-l_len // KEY_TILE) * KEY_TILE
    kc = k_cache.reshape(b, past, cw).astype(BF16)
    vc = v_cache.reshape(b, past, cw).astype(BF16)
    kic = jnp.concatenate([ki_cache, ki_cache], axis=-1).astype(BF16)
    pad = ((0, 0), (0, lp - l_len), (0, 0))
    k_all = jnp.pad(jnp.concatenate([kc, k_bf], axis=1), pad)
    v_all = jnp.pad(jnp.concatenate([vc, v_bf], axis=1), pad)
    ki_all = jnp.pad(jnp.concatenate([kic, kiki_bf], axis=1), pad)
    x = _attn(x, gt_m, q, qi, wi, k_all, v_all, ki_all, rel_bias, prm['w_out'], past, l_len)
    return (x, k.reshape(b, t, C_HEADS, C_HEAD_DIM), v.reshape(b, t, C_HEADS, C_HEAD_DIM), kiki[:, :, :IDX_DIM])


def _run_group(x, mod, shift_bufs, wkv_states, gla_states, k_caches, v_caches, ki_caches, ffn_bufs, raw, prm):
    b, t, d = x.shape
    tm = _token_tile(t)
    shifts, wkvs, glas, ks, vs, kis, ffns = [], [], [], [], [], [], []
    for i in range(DEPTH):
        m6 = [mod[i][:, None, n * d:(n + 1) * d] for n in range(N_MOD)]
        j = i // 2
        gain_mix = raw['norm_mix'][i][None]
        if i % 2 == 0:
            x, s_new, wkv_new, gla_new = _even_layer(x, m6[:3], gain_mix, shift_bufs[j], wkv_states[j],
                                                     gla_states[j], prm['even'][j])
            shifts.append(s_new)
            wkvs.append(wkv_new)
            glas.append(gla_new)
        else:
            x, k_new, v_new, ki_new = _odd_layer(x, m6[:3], gain_mix, k_caches[j], v_caches[j], ki_caches[j],
                                                 raw['rel_bias'], prm['odd'][j])
            ks.append(k_new)
            vs.append(v_new)
            kis.append(ki_new)
        f = prm['ffn'][i]
        x, f_buf = _ffn(x, raw['norm_ffn'][i][None], m6[3], m6[4], m6[5], ffn_bufs[i], f['w_up'], f['conv_w'],
                        f['conv_b'], f['w_down'], raw['norm_final'][None], tm, final=(i == DEPTH - 1))
        ffns.append(f_buf)
    return (x, jnp.stack(shifts), jnp.stack(wkvs), jnp.stack(glas), jnp.stack(ks), jnp.stack(vs),
            jnp.stack(kis), jnp.stack(ffns))


def kernel(x_prompt, x_sample, c_prompt, c_sample, state_shift_ab, state_wkv, state_gla, cache_k, cache_v, cache_kidx, state_ffn_conv, w_ada, b_ada, norm_mix, norm_ffn, norm_final, w_in_ab, mu_a, w0, w_decay_up, a0, w_iclr_up, w_gate_up, k_k, k_a, r_k, gn_a_g, gn_a_b, alpha_up, alpha_b, gn_b_g, w_out_ab, w_in_c, rel_bias, w_out_c, w_ffn_up, ffn_conv_w, ffn_conv_b, w_ffn_down):
    raw = {
        'norm_mix': norm_mix, 'norm_ffn': norm_ffn, 'norm_final': norm_final, 'rel_bias': rel_bias,
        'w_in_ab': w_in_ab, 'mu_a': mu_a, 'w0': w0, 'w_decay_up': w_decay_up, 'a0': a0, 'w_iclr_up': w_iclr_up,
        'w_gate_up': w_gate_up, 'k_k': k_k, 'k_a': k_a, 'r_k': r_k, 'gn_a_g': gn_a_g, 'gn_a_b': gn_a_b,
        'alpha_up': alpha_up, 'alpha_b': alpha_b, 'gn_b_g': gn_b_g, 'w_out_ab': w_out_ab, 'w_in_c': w_in_c,
        'w_out_c': w_out_c, 'w_ffn_up': w_ffn_up, 'ffn_conv_w': ffn_conv_w, 'ffn_conv_b': ffn_conv_b,
        'w_ffn_down': w_ffn_down,
    }
    prm = _prep_params(raw)
    n_even = w_in_ab.shape[0]
    n_odd = w_in_c.shape[0]
    dt = x_prompt.dtype
    bp = x_prompt.shape[0]
    mod = _ada(jnp.concatenate([c_prompt, c_sample], axis=0), w_ada, b_ada)
    mod_p, mod_s = mod[:, :bp], mod[:, bp:]
    out_p = _run_group(
        x_prompt, mod_p,
        jnp.zeros((n_even, bp, 1, A_COLS), dt),
        jnp.zeros((n_even, bp, A_HEADS, A_HEAD_DIM, A_HEAD_DIM), dt),
        jnp.zeros((n_even, bp, B_HEADS, B_KEY_DIM, B_VAL_DIM), dt),
        jnp.zeros((n_odd, bp, 0, C_HEADS, C_HEAD_DIM), dt),
        jnp.zeros((n_odd, bp, 0, C_HEADS, C_HEAD_DIM), dt),
        jnp.zeros((n_odd, bp, 0, IDX_DIM), dt),
        jnp.zeros((DEPTH, bp, CONV_W - 1, 2 * D_FF), dt),
        raw, prm)
    out_s = _run_group(x_sample, mod_s, state_shift_ab, state_wkv, state_gla, cache_k, cache_v, cache_kidx,
                       state_ffn_conv, raw, prm)
    res = []
    for a, b in zip(out_p, out_s):
        res.extend([a, b])
    return tuple(res)
```

```python
import functools
import math

import jax
import jax.numpy as jnp
import numpy as np
from jax import lax
from jax.experimental import pallas as pl
from jax.experimental.pallas import tpu as pltpu

F32 = jnp.float32
BF16 = jnp.bfloat16
HI = lax.Precision.HIGHEST

D_MODEL = 1024
DEPTH = 4
CHUNK = 64
A_HEADS = 8
A_HEAD_DIM = 64
A_WIDTH = A_HEADS * A_HEAD_DIM
DECAY_LORA = 32
ICLR_LORA = 32
GATE_LORA = 96
LORA_W = DECAY_LORA + ICLR_LORA + GATE_LORA
A_COLS = 3 * A_WIDTH + LORA_W
B_HEADS = 4
B_KEY_DIM = 64
B_VAL_DIM = 128
B_KEY_WIDTH = B_HEADS * B_KEY_DIM
B_WIDTH = B_HEADS * B_VAL_DIM
GLA_RANK = 16
GLA_TAU = 16.0
C_HEADS = 16
C_HEAD_DIM = 64
C_WIDTH = C_HEADS * C_HEAD_DIM
IDX_HEADS = 8
IDX_DIM = 64
TOPK_MAX = 256
QUERY_BLOCK = 128
REL_BUCKETS = 32
REL_MAX_DIST = 256
D_FF = 2816
CONV_W = 3
N_MOD = 6
RMS_EPS = 1e-6
GN_EPS = 64e-5
NEG_INF = -1e30

LANES = 128
SUBLANES = 8
VMEM_LIMIT = 56 * 1024 * 1024

LORA_PAD = 256
XAL_PAD = 128
KEY_TILE = 512
ATTN_HEAD_GROUP = 8
FOLD_ROWS = 64
VT_HEAD_ROWS = C_HEAD_DIM + 16
FFN_CHUNK = 256
RWKV_TT = 32


def _cparams(sem):
    return pltpu.CompilerParams(dimension_semantics=sem, vmem_limit_bytes=VMEM_LIMIT)


def _const_spec(shape):
    nd = len(shape)
    return pl.BlockSpec(shape, lambda *_: (0,) * nd, pipeline_mode=pl.Buffered(1))


def _sigmoid(x):
    return 1.0 / (1.0 + jnp.exp(-x))


def _silu(x):
    return x * _sigmoid(x)


def _softplus(x):
    return jnp.maximum(x, 0.0) + jnp.log(1.0 + jnp.exp(-jnp.abs(x)))


def _dot(a, b, precision=None):
    return jnp.dot(a, b, preferred_element_type=F32, precision=precision)


def _dot_nt(a, b):
    return lax.dot_general(a, b, (((1,), (1,)), ((), ())), preferred_element_type=F32)


def _seg_sum(x, ind_bf16):
    hi = x.astype(BF16)
    lo = (x - hi.astype(F32)).astype(BF16)
    return _dot(hi, ind_bf16) + _dot(lo, ind_bf16)


def _modulated_norm(x, gain, shift, scale):
    var = jnp.mean(x * x, axis=-1, keepdims=True)
    h = x * lax.rsqrt(var + RMS_EPS) * gain
    return h * (1.0 + scale) + shift


def _ada_kernel(c_ref, w_ref, b_ref, o_ref):
    c = c_ref[...]
    o_ref[0] = _dot(_silu(c), w_ref[0], HI) + b_ref[0]


def _ada(c, w_ada, b_ada):
    m, d = c.shape
    depth, _, n = w_ada.shape
    tn = 1536
    return pl.pallas_call(
        _ada_kernel,
        grid=(depth, n // tn),
        in_specs=[pl.BlockSpec((m, d), lambda l, j: (0, 0)),
                  pl.BlockSpec((1, d, tn), lambda l, j: (l, 0, j)),
                  pl.BlockSpec((1, 1, tn), lambda l, j: (l, 0, j))],
        out_specs=pl.BlockSpec((1, m, tn), lambda l, j: (l, 0, j)),
        out_shape=jax.ShapeDtypeStruct((depth, m, n), F32),
        compiler_params=_cparams(("arbitrary", "arbitrary")),
        name="ada_mod",
    )(c, w_ada, b_ada.reshape(depth, 1, n))


def _proj_kernel(x_ref, gain_ref, shift_ref, scale_ref, w_ref, *out_refs, groups):
    h = _modulated_norm(x_ref[0], gain_ref[...], shift_ref[0], scale_ref[0]).astype(BF16)
    refs = iter(out_refs)
    for c0, n, forms in groups:
        y = _dot(h, w_ref[:, c0:c0 + n])
        for dt, transposed in forms:
            next(refs)[0] = (y.T if transposed else y).astype(dt)


def _proj(x, gain, shift, scale, w, groups, tm):
    b, t, d = x.shape
    n_all = w.shape[1]
    mod_spec = pl.BlockSpec((1, 1, d), lambda i, j: (i, 0, 0))
    outs = [(n, dt, tr) for _, n, forms in groups for dt, tr in forms]
    return pl.pallas_call(
        functools.partial(_proj_kernel, groups=groups),
        grid=(b, t // tm),
        in_specs=[pl.BlockSpec((1, tm, d), lambda i, j: (i, j, 0)),
                  _const_spec((1, d)), mod_spec, mod_spec,
                  _const_spec((d, n_all))],
        out_specs=[pl.BlockSpec((1, n, tm), lambda i, j: (i, 0, j)) if tr else
                   pl.BlockSpec((1, tm, n), lambda i, j: (i, j, 0)) for n, _, tr in outs],
        out_shape=[jax.ShapeDtypeStruct((b, n, t) if tr else (b, t, n), dt) for n, dt, tr in outs],
        compiler_params=_cparams(("arbitrary", "arbitrary")),
        name="norm_proj",
    )(x, gain, shift, scale, w)


def _ffn_kernel(x_ref, gain_ref, shift_ref, scale_ref, gate_ref, buf_ref, wup_ref, cw_ref, cb_ref, wdn_ref,
                fgain_ref, xo_ref, bufo_ref, carry_ref, *, tm, final):
    t = pl.program_id(1)

    @pl.when(t == 0)
    def _():
        carry_ref[...] = buf_ref[0]

    x = x_ref[0]
    h = _modulated_norm(x, gain_ref[...], shift_ref[0], scale_ref[0]).astype(BF16)
    row = lax.broadcasted_iota(jnp.int32, (tm, FFN_CHUNK), 0)

    def conv(u, c0):
        c_old = carry_ref[0:1, c0:c0 + FFN_CHUNK]
        c_new = carry_ref[1:2, c0:c0 + FFN_CHUNK]
        prev1 = jnp.where(row == 0, c_new, pltpu.roll(u, 1, 0))
        prev2 = jnp.where(row == 0, c_old, jnp.where(row == 1, c_new, pltpu.roll(u, 2, 0)))
        carry_ref[:, c0:c0 + FFN_CHUNK] = u[tm - 2:tm, :]
        return (cb_ref[:, c0:c0 + FFN_CHUNK] + cw_ref[0:1, c0:c0 + FFN_CHUNK] * prev2
                + cw_ref[1:2, c0:c0 + FFN_CHUNK] * prev1 + cw_ref[2:3, c0:c0 + FFN_CHUNK] * u)

    acc = jnp.zeros((tm, D_MODEL), F32)
    for c in range(D_FF // FFN_CHUNK):
        g0 = c * FFN_CHUNK
        v0 = D_FF + g0
        gate = conv(_dot(h, wup_ref[:, g0:g0 + FFN_CHUNK]), g0)
        val = conv(_dot(h, wup_ref[:, v0:v0 + FFN_CHUNK]), v0)
        act = (_silu(gate) * val).astype(BF16)
        acc = acc + _dot(act, wdn_ref[g0:g0 + FFN_CHUNK, :])

    y = x + gate_ref[0] * acc
    if final:
        y = y * lax.rsqrt(jnp.mean(y * y, axis=-1, keepdims=True) + RMS_EPS) * fgain_ref[...]
    xo_ref[0] = y

    @pl.when(t == pl.num_programs(1) - 1)
    def _():
        bufo_ref[0] = carry_ref[...]


def _ffn(x, gain, shift, scale, gate, buf, w_up, conv_w, conv_b, w_down, final_gain, tm, final):
    b, t, d = x.shape
    f2 = 2 * D_FF
    mod_spec = pl.BlockSpec((1, 1, d), lambda i, j: (i, 0, 0))
    buf_spec = pl.BlockSpec((1, CONV_W - 1, f2), lambda i, j: (i, 0, 0))
    return pl.pallas_call(
        functools.partial(_ffn_kernel, tm=tm, final=final),
        grid=(b, t // tm),
        in_specs=[pl.BlockSpec((1, tm, d), lambda i, j: (i, j, 0)),
                  _const_spec((1, d)), mod_spec, mod_spec, mod_spec, buf_spec,
                  _const_spec((d, f2)), _const_spec((CONV_W, f2)), _const_spec((1, f2)),
                  _const_spec((D_FF, d)), _const_spec((1, d))],
        out_specs=[pl.BlockSpec((1, tm, d), lambda i, j: (i, j, 0)), buf_spec],
        out_shape=[jax.ShapeDtypeStruct((b, t, d), F32),
                   jax.ShapeDtypeStruct((b, CONV_W - 1, f2), F32)],
        scratch_shapes=[pltpu.VMEM((CONV_W - 1, f2), F32)],
        compiler_params=_cparams(("arbitrary", "arbitrary")),
        name="conv_ffn",
    )(x, gain, shift, scale, gate, buf, w_up, conv_w, conv_b, w_down, final_gain)


def _pre_even_kernel(rkv_ref, lx_ref, sb_rkv_ref, sb_l_ref, mu_rkv_ref, mu_l_ref, w0_ref, wd_ref, a0_ref, wi_ref,
                     wg_ref, kk_ref, ka_ref, rk_ref, ind_ref, x6_ref, g_ref, bonus_ref, c_rkv, c_l, *, tm):
    t = pl.program_id(1)

    @pl.when(t == 0)
    def _():
        c_rkv[...] = sb_rkv_ref[0]
        c_l[...] = sb_l_ref[0]

    def shifted(p, carry_ref, mu):
        row = lax.broadcasted_iota(jnp.int32, p.shape, 0)
        prev = jnp.where(row == 0, carry_ref[...], pltpu.roll(p, 1, 0))
        carry_ref[...] = p[tm - 1:tm, :]
        return p + (prev - p) * mu

    pa = shifted(rkv_ref[0], c_rkv, mu_rkv_ref[...])
    lora = shifted(lx_ref[0][:, :LORA_PAD], c_l, mu_l_ref[...])
    r = pa[:, :A_WIDTH]
    k = pa[:, A_WIDTH:2 * A_WIDTH]
    v = pa[:, 2 * A_WIDTH:]
    w_log = -_softplus(-(w0_ref[...] + _dot(jnp.tanh(lora), wd_ref[...], HI))) - 0.5
    decay = jnp.exp(-jnp.exp(w_log))
    a = _sigmoid(a0_ref[...] + _dot(lora, wi_ref[...], HI))
    g = _dot(_sigmoid(lora), wg_ref[...], HI)
    kk = k * kk_ref[...]
    k2 = k * (1.0 + (a - 1.0) * ka_ref[...])
    ind = ind_ref[...]
    kk = kk * lax.rsqrt(jnp.maximum(_seg_sum(kk * kk, ind), 1e-12))
    bonus = _seg_sum(r * k2 * rk_ref[...], ind) * v
    x6_ref[0, 0] = r
    x6_ref[1, 0] = decay
    x6_ref[2, 0] = k2
    x6_ref[3, 0] = v
    x6_ref[4, 0] = kk
    x6_ref[5, 0] = kk * a
    g_ref[0] = g
    bonus_ref[0] = bonus


def _pre_even(rkv, lx, sb_rkv, sb_l, prm, tm):
    b, t, _ = rkv.shape
    aw = A_WIDTH
    row = lambda n: _const_spec((1, n))
    return pl.pallas_call(
        functools.partial(_pre_even_kernel, tm=tm),
        grid=(b, t // tm),
        in_specs=[pl.BlockSpec((1, tm, 3 * aw), lambda i, j: (i, j, 0)),
                  pl.BlockSpec((1, tm, LORA_PAD + XAL_PAD), lambda i, j: (i, j, 0)),
                  pl.BlockSpec((1, 1, 3 * aw), lambda i, j: (i, 0, 0)),
                  pl.BlockSpec((1, 1, LORA_PAD), lambda i, j: (i, 0, 0)),
                  row(3 * aw), row(LORA_PAD), row(aw), _const_spec((LORA_PAD, aw)), row(aw),
                  _const_spec((LORA_PAD, aw)), _const_spec((LORA_PAD, aw)), row(aw), row(aw), row(aw),
                  _const_spec((aw, aw))],
        out_specs=[pl.BlockSpec((6, 1, tm, aw), lambda i, j: (0, i, j, 0)),
                   pl.BlockSpec((1, tm, aw), lambda i, j: (i, j, 0)),
                   pl.BlockSpec((1, tm, aw), lambda i, j: (i, j, 0))],
        out_shape=[jax.ShapeDtypeStruct((6, b, t, aw), F32),
                   jax.ShapeDtypeStruct((b, t, aw), F32),
                   jax.ShapeDtypeStruct((b, t, aw), F32)],
        scratch_shapes=[pltpu.VMEM((1, 3 * aw), F32), pltpu.VMEM((1, LORA_PAD), F32)],
        compiler_params=_cparams(("arbitrary", "arbitrary")),
        name="rwkv_prep",
    )(rkv, lx, sb_rkv, sb_l, prm['mu_rkv'], prm['mu_l'], prm['w0'], prm['wd'], prm['a0'], prm['wi'], prm['wg'],
      prm['k_k'], prm['k_a'], prm['r_k'], prm['ind64'])


def _rwkv_kernel(x_ref, s0_ref, y_ref, so_ref, s_scr, *, tt):
    n = A_HEAD_DIM
    g = n // SUBLANES
    i = pl.program_id(0)

    @pl.when(i == 0)
    def _():
        s_scr[...] = s0_ref[...]

    def allsum(x):
        x = x + pltpu.roll(x, 4, 0)
        x = x + pltpu.roll(x, 2, 0)
        return x + pltpu.roll(x, 1, 0)

    def ksum(x3):
        return allsum(jnp.sum(x3, axis=0))

    def step(t, carry):
        def ld(j):
            return x_ref[t, j].reshape(g, SUBLANES, LANES)
        r, w, k, kk, b = ld(0), ld(1), ld(2), ld(4), ld(5)
        wr = w * r
        br = ksum(b * r)
        kr = ksum(k * r)

        def vbody(v, c):
            sv = s_scr[v].reshape(g, SUBLANES, LANES)
            sa = ksum(sv * kk)
            z = ksum(sv * wr)
            vv = jnp.broadcast_to(x_ref[t, 3, pl.ds(v, 1), :], (SUBLANES, LANES))
            s_scr[v] = (sv * w - sa[None] * b + vv[None] * k).reshape(n, LANES)
            y = z - sa * br + vv * kr
            y_ref[t, pl.ds(v, 1), :] = y[0:1]
            return c

        lax.fori_loop(0, n, vbody, 0, unroll=4)
        return carry

    lax.fori_loop(0, tt, step, 0)

    @pl.when(i == pl.num_programs(0) - 1)
    def _():
        so_ref[...] = s_scr[...]


def _rwkv(x6t, s0, tt):
    t = x6t.shape[0]
    n = A_HEAD_DIM
    return pl.pallas_call(
        functools.partial(_rwkv_kernel, tt=tt),
        grid=(t // tt,),
        in_specs=[pl.BlockSpec((tt, 6, n, LANES), lambda i: (i, 0, 0, 0)),
                  pl.BlockSpec((n, n, LANES), lambda i: (0, 0, 0))],
        out_specs=[pl.BlockSpec((tt, n, LANES), lambda i: (i, 0, 0)),
                   pl.BlockSpec((n, n, LANES), lambda i: (0, 0, 0))],
        out_shape=[jax.ShapeDtypeStruct((t, n, LANES), F32),
                   jax.ShapeDtypeStruct((n, n, LANES), F32)],
        scratch_shapes=[pltpu.VMEM((n, n, LANES), F32)],
        compiler_params=_cparams(("arbitrary",)),
        name="rwkv_scan",
    )(x6t, s0)


def _gla_tables(c):
    idx = np.arange(c)
    t = idx[:, None]
    r = idx[None, :]
    mats = [(r <= t), (r > t)]
    masks = [np.eye(c, dtype=bool)]
    m = c // 2
    while m >= 1:
        mid = (idx // (2 * m)) * 2 * m + m - 1
        right = (idx % (2 * m)) >= m
        mats.append(right[:, None] & (r > mid[:, None]) & (r <= t))
        mats.append((~right)[:, None] & (r > t) & (r <= mid[:, None]))
        masks.append(((idx // (2 * m))[:, None] == (idx // (2 * m))[None, :]) & right[:, None] & (~right)[None, :])
        m //= 2
    return (np.concatenate(mats, 0).astype(np.float32), np.stack(masks).astype(np.float32))


def _gla_kernel(qk_ref, v_ref, lx_ref, au_ref, ab_ref, mall_ref, masks_ref, s0_ref, o_ref, so_ref, s_scr, *, c, nlev):
    i = pl.program_id(1)

    @pl.when(i == 0)
    def _():
        s_scr[...] = s0_ref[0]

    z = _dot(lx_ref[0], au_ref[...], HI) + ab_ref[...]
    g = -_softplus(-z) * (1.0 / GLA_TAU)
    d_all = _dot(mall_ref[...], g, HI)
    qk = qk_ref[0]
    q = qk[:, :B_KEY_WIDTH] * (B_KEY_DIM ** -0.5)
    k = qk[:, B_KEY_WIDTH:]
    v = v_ref[0]
    b_inc = d_all[0:c]
    b_rev = d_all[c:2 * c]
    q_in = (q * jnp.exp(b_inc)).astype(BF16)
    k_st = (k * jnp.exp(b_rev)).astype(BF16)
    dec_last = jnp.exp(b_inc[c - 1:c, :])
    qd = [q.astype(BF16)]
    kd = [k.astype(BF16)]
    for lev in range(nlev):
        dq = d_all[(2 + 2 * lev) * c:(3 + 2 * lev) * c]
        dk = d_all[(3 + 2 * lev) * c:(4 + 2 * lev) * c]
        qd.append((q * jnp.exp(dq)).astype(BF16))
        kd.append((k * jnp.exp(dk)).astype(BF16))
    for h in range(B_HEADS):
        ks = slice(h * B_KEY_DIM, (h + 1) * B_KEY_DIM)
        vs = slice(h * B_VAL_DIM, (h + 1) * B_VAL_DIM)
        att = jnp.zeros((c, c), F32)
        for lev in range(nlev + 1):
            att = att + masks_ref[lev] * _dot_nt(qd[lev][:, ks], kd[lev][:, ks])
        vh = v[:, vs]
        st = s_scr[h]
        o_ref[0, :, vs] = _dot_nt(q_in[:, ks], st.astype(BF16)) + _dot(att.astype(BF16), vh.astype(BF16))
        s_scr[h] = st * dec_last[:, ks] + _dot(vh.T.astype(BF16), k_st[:, ks])

    @pl.when(i == pl.num_programs(1) - 1)
    def _():
        so_ref[0] = s_scr[...]


def _gla(qk, v, lx, alpha_up, alpha_b, s0t):
    b, t, _ = qk.shape
    c = min(CHUNK, t)
    nlev = int(round(math.log2(c)))
    mall, masks = _gla_tables(c)
    kw, vw = B_KEY_WIDTH, B_WIDTH
    st_spec = pl.BlockSpec((1, B_HEADS, B_VAL_DIM, B_KEY_DIM), lambda i, j: (i, 0, 0, 0))
    return pl.pallas_call(
        functools.partial(_gla_kernel, c=c, nlev=nlev),
        grid=(b, t // c),
        in_specs=[pl.BlockSpec((1, c, 2 * kw), lambda i, j: (i, j, 0)),
                  pl.BlockSpec((1, c, vw), lambda i, j: (i, j, 0)),
                  pl.BlockSpec((1, c, XAL_PAD), lambda i, j: (i, j, LORA_PAD // XAL_PAD)),
                  _const_spec((XAL_PAD, kw)), _const_spec((1, kw)),
                  _const_spec(mall.shape), _const_spec(masks.shape), st_spec],
        out_specs=[pl.BlockSpec((1, c, vw), lambda i, j: (i, j, 0)), st_spec],
        out_shape=[jax.ShapeDtypeStruct((b, t, vw), F32),
                   jax.ShapeDtypeStruct((b, B_HEADS, B_VAL_DIM, B_KEY_DIM), F32)],
        scratch_shapes=[pltpu.VMEM((B_HEADS, B_VAL_DIM, B_KEY_DIM), F32)],
        compiler_params=_cparams(("arbitrary", "arbitrary")),
        name="gla_chunk",
    )(qk, v, lx, alpha_up, alpha_b, jnp.asarray(mall), jnp.asarray(masks), s0t)


def _post_even_kernel(x_ref, gate_ref, ya_ref, bonus_ref, g_ref, ob_ref, rb_ref, gag_ref, gab_ref, gbg_ref,
                      ind64_ref, ind128_ref, wout_ref, xo_ref):
    ya = ya_ref[0]
    ind64 = ind64_ref[...]
    mu = _seg_sum(ya, ind64) * (1.0 / A_HEAD_DIM)
    dev = ya - mu
    var = _seg_sum(dev * dev, ind64) * (1.0 / A_HEAD_DIM)
    out_a = (dev * lax.rsqrt(var + GN_EPS) * gag_ref[...] + gab_ref[...] + bonus_ref[0]) * g_ref[0]
    ob = ob_ref[0]
    ms = _seg_sum(ob * ob, ind128_ref[...]) * (1.0 / B_VAL_DIM)
    out_b = ob * lax.rsqrt(ms + RMS_EPS) * gbg_ref[...] * _silu(rb_ref[0])
    mix = _dot(out_a.astype(BF16), wout_ref[:A_WIDTH, :]) + _dot(out_b.astype(BF16), wout_ref[A_WIDTH:, :])
    xo_ref[0] = x_ref[0] + gate_ref[0] * mix


def _post_even(x, gate, ya, bonus, g, ob, rb, prm, tm):
    b, t, d = x.shape
    aw = A_WIDTH
    tok = lambda n: pl.BlockSpec((1, tm, n), lambda i, j: (i, j, 0))
    return pl.pallas_call(
        _post_even_kernel,
        grid=(b, t // tm),
        in_specs=[tok(d), pl.BlockSpec((1, 1, d), lambda i, j: (i, 0, 0)), tok(aw), tok(aw), tok(aw), tok(B_WIDTH),
                  tok(B_WIDTH), _const_spec((1, aw)), _const_spec((1, aw)), _const_spec((1, B_WIDTH)),
                  _const_spec((aw, aw)), _const_spec((B_WIDTH, B_WIDTH)), _const_spec((aw + B_WIDTH, d))],
        out_specs=tok(d),
        out_shape=jax.ShapeDtypeStruct((b, t, d), F32),
        compiler_params=_cparams(("arbitrary", "arbitrary")),
        name="even_out",
    )(x, gate, ya, bonus, g, ob, rb, prm['gn_a_g'], prm['gn_a_b'], prm['gn_b_g'], prm['ind64'], prm['ind128'],
      prm['w_out'])


def _fold_rows(x, op):
    return op(x.reshape(x.shape[0] // FOLD_ROWS, FOLD_ROWS, x.shape[1]), axis=0)


def _attn_kernel(x_ref, gate_ref, qt_ref, qit_ref, wit_ref, k_ref, vt_ref, ki_ref, near_ref, tri_ref, wout_ref,
                 xo_ref, key_scr, mask_scr, s_scr, ot_scr, *, past, l_len, topk):
    tq = LANES
    tk = KEY_TILE
    hg = ATTN_HEAD_GROUP
    hd = C_HEAD_DIM
    q0 = past + pl.program_id(1) * tq
    col_pos = q0 + lax.broadcasted_iota(jnp.int32, (1, tq), 1)
    lim = jnp.minimum((col_pos // CHUNK + 1) * CHUNK, l_len)
    n_adm = jnp.minimum(((q0 + tq - 1) // CHUNK + 1) * CHUNK, l_len)
    nt = (n_adm + tk - 1) // tk
    key_neg = _order_key_of(NEG_INF)
    zero_half = jnp.zeros((hd, tq), BF16)

    def head_cols(pair):
        return jnp.concatenate([jnp.concatenate([pair[:hd], zero_half], axis=0),
                                jnp.concatenate([zero_half, pair[hd:]], axis=0)], axis=1)

    w_rows = [wit_ref[0, h:h + 1, :] * (IDX_HEADS ** -0.5 * IDX_DIM ** -0.5) for h in range(IDX_HEADS)]
    qi_pairs = [head_cols(qit_ref[0, p * LANES:(p + 1) * LANES, :]) for p in range(IDX_HEADS // 2)]

    def score_tile(t, c):
        off = pl.multiple_of(t * tk, tk)
        ki = ki_ref[0, pl.ds(off, tk), :]
        score = jnp.zeros((tk, tq), F32)
        for p in range(IDX_HEADS // 2):
            s2 = _dot(ki, qi_pairs[p])
            score = score + jnp.maximum(s2[:, :tq], 0.0) * w_rows[2 * p] + jnp.maximum(s2[:, tq:], 0.0) * w_rows[2 * p + 1]
        kpos = off + lax.broadcasted_iota(jnp.int32, (tk, tq), 0)
        score = jnp.where(score == 0.0, 0.0, score)
        score = jnp.where(kpos < lim, score, NEG_INF)
        bits = lax.bitcast_convert_type(score, jnp.int32)
        key_scr[pl.ds(off, tk), :] = bits ^ ((bits >> 31) & 0x7FFFFFFF)
        return c

    lax.fori_loop(0, nt, score_tile, 0)

    def count_ge(thr):
        def body(t, acc):
            off = pl.multiple_of(t * tk, tk)
            return acc + _fold_rows(jnp.where(key_scr[pl.ds(off, tk), :] >= thr, 1.0, 0.0), jnp.sum)
        acc = lax.fori_loop(0, nt, body, jnp.zeros((FOLD_ROWS, tq), F32))
        return jnp.sum(acc, axis=0, keepdims=True)

    int_min = jnp.full((1, tq), -2 ** 31, jnp.int32)
    thr = jnp.where(count_ge(jnp.zeros((1, tq), jnp.int32)) >= topk, 0, int_min)

    def bit_step(i, thr):
        cand = thr + (jnp.int32(1) << (30 - i))
        return jnp.where(count_ge(cand) >= topk, cand, thr)

    thr = lax.fori_loop(0, 31, bit_step, thr)
    need = topk - count_ge(thr + 1)

    def mask_tile(t, run):
        off = pl.multiple_of(t * tk, tk)
        key = key_scr[pl.ds(off, tk), :]
        eq = key == thr
        prefix = run + _dot(tri_ref[...], jnp.where(eq, 1.0, 0.0).astype(BF16))
        sel = ((key > thr) | (eq & (prefix <= need))) & (key > key_neg)
        mask_scr[pl.ds(off, tk), :] = jnp.where(sel, 0.0, NEG_INF)
        return prefix[tk - 1:tk, :]

    lax.fori_loop(0, nt, mask_tile, jnp.zeros((1, tq), F32))

    jq = q0 // LANES

    def group_body(g, c):
        pair_rows = [pl.multiple_of((g * (hg // 2) + pp) * LANES, LANES) for pp in range(hg // 2)]
        q_pairs = [head_cols(qt_ref[0, pl.ds(pr, LANES), :] * (hd ** -0.5)) for pr in pair_rows]

        def logits_tile(t, cc):
            off = pl.multiple_of(t * tk, tk)
            mk = mask_scr[pl.ds(off, tk), :]
            for pp, pr in enumerate(pair_rows):
                s2 = _dot(k_ref[0, pl.ds(off, tk), pl.ds(pr, LANES)], q_pairs[pp])
                s_scr[2 * pp, pl.ds(off, tk), :] = s2[:, :tq] + mk
                s_scr[2 * pp + 1, pl.ds(off, tk), :] = s2[:, tq:] + mk
            return cc

        lax.fori_loop(0, nt, logits_tile, 0)
        for nidx in range(3):
            j = jq - 2 + nidx

            @pl.when(j >= 0)
            def _():
                off = pl.multiple_of(j * LANES, LANES)
                for hh in range(hg):
                    s_scr[hh, pl.ds(off, LANES), :] = s_scr[hh, pl.ds(off, LANES), :] + near_ref[nidx, g * hg + hh]

        def max_tile(t, ms):
            off = pl.multiple_of(t * tk, tk)
            return tuple(jnp.maximum(m, _fold_rows(s_scr[hh, pl.ds(off, tk), :], jnp.max)) for hh, m in enumerate(ms))

        ms = lax.fori_loop(0, nt, max_tile, tuple(jnp.full((FOLD_ROWS, tq), NEG_INF, F32) for _ in range(hg)))
        ms = [jnp.max(m, axis=0, keepdims=True) for m in ms]

        def pv_tile(t, accs):
            off = pl.multiple_of(t * tk, tk)
            new_accs = []
            for hh in range(hg):
                pexp = jnp.exp((s_scr[hh, pl.ds(off, tk), :] - ms[hh]).astype(BF16))
                row0 = pl.multiple_of((g * hg + hh) * VT_HEAD_ROWS, 16)
                new_accs.append(accs[hh] + _dot(vt_ref[0, pl.ds(row0, VT_HEAD_ROWS), pl.ds(off, tk)], pexp))
            return tuple(new_accs)

        accs = lax.fori_loop(0, nt, pv_tile, tuple(jnp.zeros((VT_HEAD_ROWS, tq), F32) for _ in range(hg)))
        for hh in range(hg):
            row0 = pl.multiple_of((g * hg + hh) * hd, hd)
            ot_scr[pl.ds(row0, hd), :] = accs[hh][:hd] / accs[hh][hd:hd + 1]
        return c

    lax.fori_loop(0, C_HEADS // hg, group_body, 0)
    mix = _dot(ot_scr[...].T.astype(BF16), wout_ref[...])
    xo_ref[0] = x_ref[0] + gate_ref[0] * mix


def _order_key_of(value):
    bits = int(np.float32(value).view(np.int32))
    return bits ^ ((bits >> 31) & 0x7FFFFFFF)


def _rel_bucket(rel):
    half = REL_BUCKETS // 2
    max_exact = half // 2
    n = jnp.abs(rel)
    nf = jnp.maximum(n, 1).astype(F32)
    large = max_exact + (jnp.log(nf / max_exact) / math.log(REL_MAX_DIST / max_exact)
                         * (half - max_exact)).astype(jnp.int32)
    large = jnp.minimum(large, half - 1)
    return jnp.where(rel > 0, half, 0) + jnp.where(n < max_exact, n, large)


def _attn(x, gate, qt, qit, wit, k_all, vt_all, ki_all, rel_bias, w_out, past, l_len):
    b, t, d = x.shape
    lp = k_all.shape[1]
    tq = LANES
    topk = min(TOPK_MAX, l_len // 4)
    assert t % tq == 0 and past % LANES == 0 and lp % KEY_TILE == 0 and lp >= l_len
    assert 3 * LANES - (LANES - 1) >= REL_MAX_DIST
    rel = ((jnp.arange(3) - 2) * LANES)[:, None, None] + jnp.arange(LANES)[None, :, None] - jnp.arange(tq)[None, None, :]
    far = rel_bias[_rel_bucket(jnp.int32(-REL_MAX_DIST))]
    near = jnp.transpose(rel_bias[_rel_bucket(rel)] - far, (0, 3, 1, 2)).astype(F32)
    tri = jnp.asarray(np.tril(np.ones((KEY_TILE, KEY_TILE), np.float32)), BF16)
    tok = lambda n: pl.BlockSpec((1, tq, n), lambda i, j: (i, j, 0))
    tok_t = lambda n: pl.BlockSpec((1, n, tq), lambda i, j: (i, 0, j))
    resident = lambda shape: pl.BlockSpec((1,) + shape, lambda i, j: (i, 0, 0), pipeline_mode=pl.Buffered(1))
    return pl.pallas_call(
        functools.partial(_attn_kernel, past=past, l_len=l_len, topk=topk),
        grid=(b, t // tq),
        in_specs=[tok(d), pl.BlockSpec((1, 1, d), lambda i, j: (i, 0, 0)), tok_t(C_WIDTH), tok_t(IDX_HEADS * IDX_DIM),
                  tok_t(IDX_HEADS), resident((lp, C_WIDTH)), resident((C_HEADS * VT_HEAD_ROWS, lp)),
                  resident((lp, LANES)),
                  _const_spec((3, C_HEADS, LANES, tq)), _const_spec((KEY_TILE, KEY_TILE)),
                  _const_spec((C_WIDTH, d))],
        out_specs=tok(d),
        out_shape=jax.ShapeDtypeStruct((b, t, d), F32),
        scratch_shapes=[pltpu.VMEM((lp, tq), jnp.int32), pltpu.VMEM((lp, tq), F32),
                        pltpu.VMEM((ATTN_HEAD_GROUP, lp, tq), F32), pltpu.VMEM((C_WIDTH, tq), F32)],
        compiler_params=_cparams(("arbitrary", "arbitrary")),
        name="dsa_attn",
    )(x, gate, qt, qit, wit, k_all, vt_all, ki_all, near, tri, w_out)


def _prep_params(p):
    d = D_MODEL
    aw = A_WIDTH
    out = {'even': [], 'odd': [], 'ffn': []}
    ind64 = jnp.asarray(np.kron(np.eye(aw // A_HEAD_DIM), np.ones((A_HEAD_DIM, A_HEAD_DIM))), BF16)
    ind128 = jnp.asarray(np.kron(np.eye(B_WIDTH // B_VAL_DIM), np.ones((B_VAL_DIM, B_VAL_DIM))), BF16)
    for j in range(p['w_in_ab'].shape[0]):
        w = p['w_in_ab'][j]
        o = A_COLS
        lora = jnp.pad(w[:, 3 * aw:A_COLS], ((0, 0), (0, LORA_PAD - LORA_W)))
        xal = jnp.pad(w[:, o + 2 * B_KEY_WIDTH + B_WIDTH:o + 2 * B_KEY_WIDTH + B_WIDTH + GLA_RANK],
                      ((0, 0), (0, XAL_PAD - GLA_RANK)))
        w_in = jnp.concatenate([w[:, :3 * aw], w[:, o:o + 2 * B_KEY_WIDTH + B_WIDTH],
                                w[:, o + 2 * B_KEY_WIDTH + B_WIDTH + GLA_RANK:], lora, xal], axis=1).astype(BF16)
        mu = p['mu_a'][j]

        def lora_rows(m, r0):
            return jnp.pad(m, ((r0, LORA_PAD - r0 - m.shape[0]), (0, 0)))

        out['even'].append({
            'w_in': w_in,
            'mu_rkv': mu[None, :3 * aw],
            'mu_l': jnp.pad(mu[None, 3 * aw:], ((0, 0), (0, LORA_PAD - LORA_W))),
            'w0': p['w0'][j][None], 'a0': p['a0'][j][None],
            'wd': lora_rows(p['w_decay_up'][j], 0),
            'wi': lora_rows(p['w_iclr_up'][j], DECAY_LORA),
            'wg': lora_rows(p['w_gate_up'][j], DECAY_LORA + ICLR_LORA),
            'k_k': p['k_k'][j][None], 'k_a': p['k_a'][j][None], 'r_k': p['r_k'][j].reshape(1, aw),
            'gn_a_g': p['gn_a_g'][j][None], 'gn_a_b': p['gn_a_b'][j][None], 'gn_b_g': p['gn_b_g'][j][None],
            'alpha_up': jnp.pad(p['alpha_up'][j], ((0, XAL_PAD - GLA_RANK), (0, 0))),
            'alpha_b': p['alpha_b'][j][None],
            'w_out': p['w_out_ab'][j].astype(BF16),
            'ind64': ind64, 'ind128': ind128,
        })
    for j in range(p['w_in_c'].shape[0]):
        w = p['w_in_c'][j]
        o4 = 3 * C_WIDTH + IDX_HEADS * IDX_DIM
        ki = w[:, o4:o4 + IDX_DIM]
        wi = jnp.pad(w[:, o4 + IDX_DIM:], ((0, 0), (0, LANES - IDX_HEADS)))
        out['odd'].append({
            'w_in': jnp.concatenate([w[:, :o4], ki, ki, wi], axis=1).astype(BF16),
            'w_out': p['w_out_c'][j].astype(BF16),
        })
    for i in range(DEPTH):
        out['ffn'].append({
            'w_up': p['w_ffn_up'][i].astype(BF16), 'conv_w': p['ffn_conv_w'][i], 'conv_b': p['ffn_conv_b'][i][None],
            'w_down': p['w_ffn_down'][i].astype(BF16),
        })
    return out


def _token_tile(t):
    return min(512, t)


def _even_layer(x, mods, gain, shift_buf, wkv0, gla0, prm):
    b, t, d = x.shape
    aw = A_WIDTH
    tm = _token_tile(t)
    sh_m, sc_m, gt_m = mods
    kw2 = 2 * B_KEY_WIDTH
    plain = ((F32, False),)
    cols = [(0, 3 * aw, plain), (3 * aw, kw2, plain), (3 * aw + kw2, B_WIDTH, plain),
            (3 * aw + kw2 + B_WIDTH, B_WIDTH, plain), (3 * aw + kw2 + 2 * B_WIDTH, LORA_PAD + XAL_PAD, plain)]
    rkv, qk_b, v_b, r_b, lx = _proj(x, gain, sh_m, sc_m, prm['w_in'], cols, tm)
    new_shift = jnp.concatenate([rkv[:, t - 1:, :], lx[:, t - 1:, :LORA_W]], axis=-1)
    sb_rkv = shift_buf[:, :, :3 * aw]
    sb_l = jnp.pad(shift_buf[:, :, 3 * aw:], ((0, 0), (0, 0), (0, LORA_PAD - LORA_W)))
    x6, g, bonus = _pre_even(rkv, lx, sb_rkv, sb_l, prm, tm)
    chains = b * A_HEADS
    assert chains == LANES
    x6t = jnp.transpose(x6.reshape(6, b, t, A_HEADS, A_HEAD_DIM), (2, 0, 4, 1, 3)).reshape(t, 6, A_HEAD_DIM, chains)
    s0 = jnp.transpose(wkv0, (2, 3, 0, 1)).reshape(A_HEAD_DIM, A_HEAD_DIM, chains)
    ya_t, s_fin = _rwkv(x6t, s0, min(RWKV_TT, t))
    ya = jnp.transpose(ya_t.reshape(t, A_HEAD_DIM, b, A_HEADS), (2, 0, 3, 1)).reshape(b, t, aw)
    wkv_new = jnp.transpose(s_fin.reshape(A_HEAD_DIM, A_HEAD_DIM, b, A_HEADS), (2, 3, 0, 1))
    ob, gla_t = _gla(qk_b, v_b, lx, prm['alpha_up'], prm['alpha_b'], jnp.swapaxes(gla0, 2, 3))
    gla_new = jnp.swapaxes(gla_t, 2, 3)
    x = _post_even(x, gt_m, ya, bonus, g, ob, r_b, prm, tm)
    return x, new_shift, wkv_new, gla_new


def _odd_layer(x, mods, gain, k_cache, v_cache, ki_cache, rel_bias, prm):
    b, t, d = x.shape
    tp = -(-t // LANES) * LANES
    xp = jnp.pad(x, ((0, 0), (0, tp - t), (0, 0)))
    tm = _token_tile(tp)
    sh_m, sc_m, gt_m = mods
    cw = C_WIDTH
    iw = IDX_HEADS * IDX_DIM
    cols = [(0, cw, ((BF16, True),)),
            (cw, cw, ((F32, False), (BF16, False))),
            (2 * cw, cw, ((F32, False), (BF16, True))),
            (3 * cw, iw, ((BF16, True),)),
            (3 * cw + iw, LANES, ((F32, False), (BF16, False))),
            (3 * cw + iw + LANES, LANES, ((F32, True),))]
    qt, k, k_bf, v, vt_bf, qit, kiki, kiki_bf, wit = _proj(xp, gain, sh_m, sc_m, prm['w_in'], cols, tm)
    past = k_cache.shape[1]
    l_len = past + t
    lp = -(-l_len // KEY_TILE) * KEY_TILE
    kc = k_cache.reshape(b, past, cw).astype(BF16)
    vct = jnp.swapaxes(v_cache.reshape(b, past, cw).astype(BF16), 1, 2)
    kic = jnp.concatenate([ki_cache, ki_cache], axis=-1).astype(BF16)
    pad = ((0, 0), (0, lp - l_len), (0, 0))
    k_all = jnp.pad(jnp.concatenate([kc, k_bf[:, :t]], axis=1), pad)
    vt_heads = jnp.concatenate([vct, vt_bf[:, :, :t]], axis=2).reshape(b, C_HEADS, C_HEAD_DIM, l_len)
    extra = jnp.zeros((b, C_HEADS, VT_HEAD_ROWS - C_HEAD_DIM, l_len), BF16).at[:, :, 0].set(1.0)
    vt_all = jnp.concatenate([vt_heads, extra], axis=2).reshape(b, C_HEADS * VT_HEAD_ROWS, l_len)
    vt_all = jnp.pad(vt_all, ((0, 0), (0, 0), (0, lp - l_len)))
    ki_all = jnp.pad(jnp.concatenate([kic, kiki_bf[:, :t]], axis=1), pad)
    xo = _attn(xp, gt_m, qt, qit, wit, k_all, vt_all, ki_all, rel_bias, prm['w_out'], past, l_len)
    return (xo[:, :t], k[:, :t].reshape(b, t, C_HEADS, C_HEAD_DIM), v[:, :t].reshape(b, t, C_HEADS, C_HEAD_DIM),
            kiki[:, :t, :IDX_DIM])


def _run_group(x, mod, shift_bufs, wkv_states, gla_states, k_caches, v_caches, ki_caches, ffn_bufs, raw, prm):
    b, t, d = x.shape
    tm = _token_tile(t)
    shifts, wkvs, glas, ks, vs, kis, ffns = [], [], [], [], [], [], []
    for i in range(DEPTH):
        m6 = [mod[i][:, None, n * d:(n + 1) * d] for n in range(N_MOD)]
        j = i // 2
        gain_mix = raw['norm_mix'][i][None]
        if i % 2 == 0:
            x, s_new, wkv_new, gla_new = _even_layer(x, m6[:3], gain_mix, shift_bufs[j], wkv_states[j],
                                                     gla_states[j], prm['even'][j])
            shifts.append(s_new)
            wkvs.append(wkv_new)
            glas.append(gla_new)
        else:
            x, k_new, v_new, ki_new = _odd_layer(x, m6[:3], gain_mix, k_caches[j], v_caches[j], ki_caches[j],
                                                 raw['rel_bias'], prm['odd'][j])
            ks.append(k_new)
            vs.append(v_new)
            kis.append(ki_new)
        f = prm['ffn'][i]
        x, f_buf = _ffn(x, raw['norm_ffn'][i][None], m6[3], m6[4], m6[5], ffn_bufs[i], f['w_up'], f['conv_w'],
                        f['conv_b'], f['w_down'], raw['norm_final'][None], tm, final=(i == DEPTH - 1))
        ffns.append(f_buf)
    return (x, jnp.stack(shifts), jnp.stack(wkvs), jnp.stack(glas), jnp.stack(ks), jnp.stack(vs),
            jnp.stack(kis), jnp.stack(ffns))


def kernel(x_prompt, x_sample, c_prompt, c_sample, state_shift_ab, state_wkv, state_gla, cache_k, cache_v, cache_kidx, state_ffn_conv, w_ada, b_ada, norm_mix, norm_ffn, norm_final, w_in_ab, mu_a, w0, w_decay_up, a0, w_iclr_up, w_gate_up, k_k, k_a, r_k, gn_a_g, gn_a_b, alpha_up, alpha_b, gn_b_g, w_out_ab, w_in_c, rel_bias, w_out_c, w_ffn_up, ffn_conv_w, ffn_conv_b, w_ffn_down):
    raw = {
        'norm_mix': norm_mix, 'norm_ffn': norm_ffn, 'norm_final': norm_final, 'rel_bias': rel_bias,
        'w_in_ab': w_in_ab, 'mu_a': mu_a, 'w0': w0, 'w_decay_up': w_decay_up, 'a0': a0, 'w_iclr_up': w_iclr_up,
        'w_gate_up': w_gate_up, 'k_k': k_k, 'k_a': k_a, 'r_k': r_k, 'gn_a_g': gn_a_g, 'gn_a_b': gn_a_b,
        'alpha_up': alpha_up, 'alpha_b': alpha_b, 'gn_b_g': gn_b_g, 'w_out_ab': w_out_ab, 'w_in_c': w_in_c,
        'w_out_c': w_out_c, 'w_ffn_up': w_ffn_up, 'ffn_conv_w': ffn_conv_w, 'ffn_conv_b': ffn_conv_b,
        'w_ffn_down': w_ffn_down,
    }
    prm = _prep_params(raw)
    n_even = w_in_ab.shape[0]
    n_odd = w_in_c.shape[0]
    dt = x_prompt.dtype
    bp = x_prompt.shape[0]
    mod = _ada(jnp.concatenate([c_prompt, c_sample], axis=0), w_ada, b_ada)
    mod_p, mod_s = mod[:, :bp], mod[:, bp:]
    out_p = _run_group(
        x_prompt, mod_p,
        jnp.zeros((n_even, bp, 1, A_COLS), dt),
        jnp.zeros((n_even, bp, A_HEADS, A_HEAD_DIM, A_HEAD_DIM), dt),
        jnp.zeros((n_even, bp, B_HEADS, B_KEY_DIM, B_VAL_DIM), dt),
        jnp.zeros((n_odd, bp, 0, C_HEADS, C_HEAD_DIM), dt),
        jnp.zeros((n_odd, bp, 0, C_HEADS, C_HEAD_DIM), dt),
        jnp.zeros((n_odd, bp, 0, IDX_DIM), dt),
        jnp.zeros((DEPTH, bp, CONV_W - 1, 2 * D_FF), dt),
        raw, prm)
    out_s = _run_group(x_sample, mod_s, state_shift_ab, state_wkv, state_gla, cache_k, cache_v, cache_kidx,
                       state_ffn_conv, raw, prm)
    res = []
    for a, b in zip(out_p, out_s):
        res.extend([a, b])
    return tuple(res)
```

```python
import functools
import math

import jax
import jax.numpy as jnp
import numpy as np
from jax import lax
from jax.experimental import pallas as pl
from jax.experimental.pallas import tpu as pltpu

F32 = jnp.float32
BF16 = jnp.bfloat16
HI = lax.Precision.HIGHEST

D_MODEL = 1024
DEPTH = 4
CHUNK = 64
A_HEADS = 8
A_HEAD_DIM = 64
A_WIDTH = A_HEADS * A_HEAD_DIM
DECAY_LORA = 32
ICLR_LORA = 32
GATE_LORA = 96
LORA_W = DECAY_LORA + ICLR_LORA + GATE_LORA
A_COLS = 3 * A_WIDTH + LORA_W
B_HEADS = 4
B_KEY_DIM = 64
B_VAL_DIM = 128
B_KEY_WIDTH = B_HEADS * B_KEY_DIM
B_WIDTH = B_HEADS * B_VAL_DIM
GLA_RANK = 16
GLA_TAU = 16.0
C_HEADS = 16
C_HEAD_DIM = 64
C_WIDTH = C_HEADS * C_HEAD_DIM
IDX_HEADS = 8
IDX_DIM = 64
TOPK_MAX = 256
QUERY_BLOCK = 128
REL_BUCKETS = 32
REL_MAX_DIST = 256
D_FF = 2816
CONV_W = 3
N_MOD = 6
RMS_EPS = 1e-6
GN_EPS = 64e-5
NEG_INF = -1e30

LANES = 128
SUBLANES = 8
VMEM_LIMIT = 56 * 1024 * 1024

LORA_PAD = 256
XAL_PAD = 128
KEY_TILE = 512
ATTN_HEAD_GROUP = 8
FOLD_ROWS = 64
VT_HEAD_ROWS = C_HEAD_DIM + 16
FFN_CHUNK = 256
FFN_TOKEN_TILE = 1024
RWKV_TT = 32
GLA_CHUNKS_PER_STEP = 4


def _cparams(sem):
    return pltpu.CompilerParams(dimension_semantics=sem, vmem_limit_bytes=VMEM_LIMIT)


def _const_spec(shape):
    nd = len(shape)
    return pl.BlockSpec(shape, lambda *_: (0,) * nd, pipeline_mode=pl.Buffered(1))


def _sigmoid(x):
    return 1.0 / (1.0 + jnp.exp(-x))


def _silu(x):
    return x * _sigmoid(x)


def _softplus(x):
    return jnp.maximum(x, 0.0) + jnp.log(1.0 + jnp.exp(-jnp.abs(x)))


def _dot(a, b, precision=None):
    return jnp.dot(a, b, preferred_element_type=F32, precision=precision)


def _dot_nt(a, b):
    return lax.dot_general(a, b, (((1,), (1,)), ((), ())), preferred_element_type=F32)


def _seg_sum(x, ind_bf16):
    hi = x.astype(BF16)
    lo = (x - hi.astype(F32)).astype(BF16)
    return _dot(hi, ind_bf16) + _dot(lo, ind_bf16)


def _modulated_norm(x, gain, shift, scale):
    var = jnp.mean(x * x, axis=-1, keepdims=True)
    h = x * lax.rsqrt(var + RMS_EPS) * gain
    return h * (1.0 + scale) + shift


def _ada_kernel(c_ref, w_ref, b_ref, o_ref):
    c = c_ref[...]
    o_ref[0] = _dot(_silu(c), w_ref[0], HI) + b_ref[0]


def _ada(c, w_ada, b_ada):
    m, d = c.shape
    depth, _, n = w_ada.shape
    tn = 1536
    return pl.pallas_call(
        _ada_kernel,
        grid=(depth, n // tn),
        in_specs=[pl.BlockSpec((m, d), lambda l, j: (0, 0)),
                  pl.BlockSpec((1, d, tn), lambda l, j: (l, 0, j)),
                  pl.BlockSpec((1, 1, tn), lambda l, j: (l, 0, j))],
        out_specs=pl.BlockSpec((1, m, tn), lambda l, j: (l, 0, j)),
        out_shape=jax.ShapeDtypeStruct((depth, m, n), F32),
        compiler_params=_cparams(("arbitrary", "arbitrary")),
        name="ada_mod",
    )(c, w_ada, b_ada.reshape(depth, 1, n))


def _proj_kernel(x_ref, gain_ref, shift_ref, scale_ref, w_ref, *out_refs, groups):
    h = _modulated_norm(x_ref[0], gain_ref[...], shift_ref[0], scale_ref[0]).astype(BF16)
    refs = iter(out_refs)
    for c0, n, forms in groups:
        y = _dot(h, w_ref[:, c0:c0 + n])
        for dt, transposed in forms:
            next(refs)[0] = (y.T if transposed else y).astype(dt)


def _proj(x, gain, shift, scale, w, groups, tm):
    b, t, d = x.shape
    n_all = w.shape[1]
    mod_spec = pl.BlockSpec((1, 1, d), lambda i, j: (i, 0, 0))
    outs = [(n, dt, tr) for _, n, forms in groups for dt, tr in forms]
    return pl.pallas_call(
        functools.partial(_proj_kernel, groups=groups),
        grid=(b, t // tm),
        in_specs=[pl.BlockSpec((1, tm, d), lambda i, j: (i, j, 0)),
                  _const_spec((1, d)), mod_spec, mod_spec,
                  _const_spec((d, n_all))],
        out_specs=[pl.BlockSpec((1, n, tm), lambda i, j: (i, 0, j)) if tr else
                   pl.BlockSpec((1, tm, n), lambda i, j: (i, j, 0)) for n, _, tr in outs],
        out_shape=[jax.ShapeDtypeStruct((b, n, t) if tr else (b, t, n), dt) for n, dt, tr in outs],
        compiler_params=_cparams(("arbitrary", "arbitrary")),
        name="norm_proj",
    )(x, gain, shift, scale, w)


def _ffn_kernel(x_ref, gain_ref, shift_ref, scale_ref, gate_ref, buf_ref, wup_ref, cw_ref, cb_ref, wdn_ref,
                fgain_ref, xo_ref, bufo_ref, carry_ref, *, tm, final):
    t = pl.program_id(1)

    @pl.when(t == 0)
    def _():
        carry_ref[...] = buf_ref[0]

    x = x_ref[0]
    h = _modulated_norm(x, gain_ref[...], shift_ref[0], scale_ref[0]).astype(BF16)
    row = lax.broadcasted_iota(jnp.int32, (tm, FFN_CHUNK), 0)

    def conv(u, c0):
        c_old = carry_ref[0:1, c0:c0 + FFN_CHUNK]
        c_new = carry_ref[1:2, c0:c0 + FFN_CHUNK]
        prev1 = jnp.where(row == 0, c_new, pltpu.roll(u, 1, 0))
        prev2 = jnp.where(row == 0, c_old, jnp.where(row == 1, c_new, pltpu.roll(u, 2, 0)))
        carry_ref[:, c0:c0 + FFN_CHUNK] = u[tm - 2:tm, :]
        return (cb_ref[:, c0:c0 + FFN_CHUNK] + cw_ref[0:1, c0:c0 + FFN_CHUNK] * prev2
                + cw_ref[1:2, c0:c0 + FFN_CHUNK] * prev1 + cw_ref[2:3, c0:c0 + FFN_CHUNK] * u)

    acc = jnp.zeros((tm, D_MODEL), F32)
    for c in range(D_FF // FFN_CHUNK):
        g0 = c * FFN_CHUNK
        v0 = D_FF + g0
        gate = conv(_dot(h, wup_ref[:, g0:g0 + FFN_CHUNK]), g0)
        val = conv(_dot(h, wup_ref[:, v0:v0 + FFN_CHUNK]), v0)
        act = (_silu(gate) * val).astype(BF16)
        acc = acc + _dot(act, wdn_ref[g0:g0 + FFN_CHUNK, :])

    y = x + gate_ref[0] * acc
    if final:
        y = y * lax.rsqrt(jnp.mean(y * y, axis=-1, keepdims=True) + RMS_EPS) * fgain_ref[...]
    xo_ref[0] = y

    @pl.when(t == pl.num_programs(1) - 1)
    def _():
        bufo_ref[0] = carry_ref[...]


def _ffn(x, gain, shift, scale, gate, buf, w_up, conv_w, conv_b, w_down, final_gain, tm, final):
    b, t, d = x.shape
    f2 = 2 * D_FF
    mod_spec = pl.BlockSpec((1, 1, d), lambda i, j: (i, 0, 0))
    buf_spec = pl.BlockSpec((1, CONV_W - 1, f2), lambda i, j: (i, 0, 0))
    return pl.pallas_call(
        functools.partial(_ffn_kernel, tm=tm, final=final),
        grid=(b, t // tm),
        in_specs=[pl.BlockSpec((1, tm, d), lambda i, j: (i, j, 0)),
                  _const_spec((1, d)), mod_spec, mod_spec, mod_spec, buf_spec,
                  _const_spec((d, f2)), _const_spec((CONV_W, f2)), _const_spec((1, f2)),
                  _const_spec((D_FF, d)), _const_spec((1, d))],
        out_specs=[pl.BlockSpec((1, tm, d), lambda i, j: (i, j, 0)), buf_spec],
        out_shape=[jax.ShapeDtypeStruct((b, t, d), F32),
                   jax.ShapeDtypeStruct((b, CONV_W - 1, f2), F32)],
        scratch_shapes=[pltpu.VMEM((CONV_W - 1, f2), F32)],
        compiler_params=_cparams(("arbitrary", "arbitrary")),
        name="conv_ffn",
    )(x, gain, shift, scale, gate, buf, w_up, conv_w, conv_b, w_down, final_gain)


def _pre_even_kernel(rkv_ref, lx_ref, sb_rkv_ref, sb_l_ref, mu_rkv_ref, mu_l_ref, w0_ref, wd_ref, a0_ref, wi_ref,
                     wg_ref, kk_ref, ka_ref, rk_ref, ind_ref, x6_ref, g_ref, bonus_ref, c_rkv, c_l, *, tm):
    t = pl.program_id(1)

    @pl.when(t == 0)
    def _():
        c_rkv[...] = sb_rkv_ref[0]
        c_l[...] = sb_l_ref[0]

    def shifted(p, carry_ref, mu):
        row = lax.broadcasted_iota(jnp.int32, p.shape, 0)
        prev = jnp.where(row == 0, carry_ref[...], pltpu.roll(p, 1, 0))
        carry_ref[...] = p[tm - 1:tm, :]
        return p + (prev - p) * mu

    pa = shifted(rkv_ref[0], c_rkv, mu_rkv_ref[...])
    lora = shifted(lx_ref[0][:, :LORA_PAD], c_l, mu_l_ref[...])
    r = pa[:, :A_WIDTH]
    k = pa[:, A_WIDTH:2 * A_WIDTH]
    v = pa[:, 2 * A_WIDTH:]
    w_log = -_softplus(-(w0_ref[...] + _dot(jnp.tanh(lora), wd_ref[...], HI))) - 0.5
    decay = jnp.exp(-jnp.exp(w_log))
    a = _sigmoid(a0_ref[...] + _dot(lora, wi_ref[...], HI))
    g = _dot(_sigmoid(lora), wg_ref[...], HI)
    kk = k * kk_ref[...]
    k2 = k * (1.0 + (a - 1.0) * ka_ref[...])
    ind = ind_ref[...]
    kk = kk * lax.rsqrt(jnp.maximum(_seg_sum(kk * kk, ind), 1e-12))
    bonus = _seg_sum(r * k2 * rk_ref[...], ind) * v
    x6_ref[0, 0] = r
    x6_ref[1, 0] = decay
    x6_ref[2, 0] = k2
    x6_ref[3, 0] = v
    x6_ref[4, 0] = kk
    x6_ref[5, 0] = kk * a
    g_ref[0] = g
    bonus_ref[0] = bonus


def _pre_even(rkv, lx, sb_rkv, sb_l, prm, tm):
    b, t, _ = rkv.shape
    aw = A_WIDTH
    row = lambda n: _const_spec((1, n))
    return pl.pallas_call(
        functools.partial(_pre_even_kernel, tm=tm),
        grid=(b, t // tm),
        in_specs=[pl.BlockSpec((1, tm, 3 * aw), lambda i, j: (i, j, 0)),
                  pl.BlockSpec((1, tm, LORA_PAD + XAL_PAD), lambda i, j: (i, j, 0)),
                  pl.BlockSpec((1, 1, 3 * aw), lambda i, j: (i, 0, 0)),
                  pl.BlockSpec((1, 1, LORA_PAD), lambda i, j: (i, 0, 0)),
                  row(3 * aw), row(LORA_PAD), row(aw), _const_spec((LORA_PAD, aw)), row(aw),
                  _const_spec((LORA_PAD, aw)), _const_spec((LORA_PAD, aw)), row(aw), row(aw), row(aw),
                  _const_spec((aw, aw))],
        out_specs=[pl.BlockSpec((6, 1, tm, aw), lambda i, j: (0, i, j, 0)),
                   pl.BlockSpec((1, tm, aw), lambda i, j: (i, j, 0)),
                   pl.BlockSpec((1, tm, aw), lambda i, j: (i, j, 0))],
        out_shape=[jax.ShapeDtypeStruct((6, b, t, aw), F32),
                   jax.ShapeDtypeStruct((b, t, aw), F32),
                   jax.ShapeDtypeStruct((b, t, aw), F32)],
        scratch_shapes=[pltpu.VMEM((1, 3 * aw), F32), pltpu.VMEM((1, LORA_PAD), F32)],
        compiler_params=_cparams(("arbitrary", "arbitrary")),
        name="rwkv_prep",
    )(rkv, lx, sb_rkv, sb_l, prm['mu_rkv'], prm['mu_l'], prm['w0'], prm['wd'], prm['a0'], prm['wi'], prm['wg'],
      prm['k_k'], prm['k_a'], prm['r_k'], prm['ind64'])


def _rwkv_kernel(x_ref, s0_ref, y_ref, so_ref, s_scr, *, tt):
    n = A_HEAD_DIM
    g = n // SUBLANES
    i = pl.program_id(0)

    @pl.when(i == 0)
    def _():
        s_scr[...] = s0_ref[...]

    def allsum(x):
        x = x + pltpu.roll(x, 4, 0)
        x = x + pltpu.roll(x, 2, 0)
        return x + pltpu.roll(x, 1, 0)

    def ksum(x3):
        return allsum(jnp.sum(x3, axis=0))

    def step(t, carry):
        def ld(j):
            return x_ref[t, j].reshape(g, SUBLANES, LANES)
        r, w, k, kk, b = ld(0), ld(1), ld(2), ld(4), ld(5)
        wr = w * r
        br = ksum(b * r)
        kr = ksum(k * r)

        def vbody(v, c):
            sv = s_scr[v].reshape(g, SUBLANES, LANES)
            sa = ksum(sv * kk)
            z = ksum(sv * wr)
            vv = jnp.broadcast_to(x_ref[t, 3, pl.ds(v, 1), :], (SUBLANES, LANES))
            s_scr[v] = (sv * w - sa[None] * b + vv[None] * k).reshape(n, LANES)
            y = z - sa * br + vv * kr
            y_ref[t, pl.ds(v, 1), :] = y[0:1]
            return c

        lax.fori_loop(0, n, vbody, 0, unroll=8)
        return carry

    lax.fori_loop(0, tt, step, 0)

    @pl.when(i == pl.num_programs(0) - 1)
    def _():
        so_ref[...] = s_scr[...]


def _rwkv(x6t, s0, tt):
    t = x6t.shape[0]
    n = A_HEAD_DIM
    return pl.pallas_call(
        functools.partial(_rwkv_kernel, tt=tt),
        grid=(t // tt,),
        in_specs=[pl.BlockSpec((tt, 6, n, LANES), lambda i: (i, 0, 0, 0)),
                  pl.BlockSpec((n, n, LANES), lambda i: (0, 0, 0))],
        out_specs=[pl.BlockSpec((tt, n, LANES), lambda i: (i, 0, 0)),
                   pl.BlockSpec((n, n, LANES), lambda i: (0, 0, 0))],
        out_shape=[jax.ShapeDtypeStruct((t, n, LANES), F32),
                   jax.ShapeDtypeStruct((n, n, LANES), F32)],
        scratch_shapes=[pltpu.VMEM((n, n, LANES), F32)],
        compiler_params=_cparams(("arbitrary",)),
        name="rwkv_scan",
    )(x6t, s0)


def _gla_tables(c):
    idx = np.arange(c)
    t = idx[:, None]
    r = idx[None, :]
    mats = [(r <= t), (r > t)]
    masks = [np.eye(c, dtype=bool)]
    m = c // 2
    while m >= 1:
        mid = (idx // (2 * m)) * 2 * m + m - 1
        right = (idx % (2 * m)) >= m
        mats.append(right[:, None] & (r > mid[:, None]) & (r <= t))
        mats.append((~right)[:, None] & (r > t) & (r <= mid[:, None]))
        masks.append(((idx // (2 * m))[:, None] == (idx // (2 * m))[None, :]) & right[:, None] & (~right)[None, :])
        m //= 2
    return (np.concatenate(mats, 0).astype(np.float32), np.stack(masks).astype(np.float32))


def _gla_kernel(qk_ref, v_ref, lx_ref, au_ref, ab_ref, mall_ref, masks_ref, s0_ref, o_ref, so_ref, s_scr, *, c, nlev,
                nc):
    i = pl.program_id(1)

    @pl.when(i == 0)
    def _():
        s_scr[...] = s0_ref[0]

    z_all = _dot(lx_ref[0], au_ref[...], HI) + ab_ref[...]
    g_all = -_softplus(-z_all) * (1.0 / GLA_TAU)
    mall = mall_ref[...]
    for j in range(nc):
        rows = slice(j * c, (j + 1) * c)
        g = g_all[rows]
        g_hi = g.astype(BF16)
        g_mid = (g - g_hi.astype(F32)).astype(BF16)
        g_lo = (g - g_hi.astype(F32) - g_mid.astype(F32)).astype(BF16)
        d_all = _dot(mall, g_hi) + _dot(mall, g_mid) + _dot(mall, g_lo)
        qk = qk_ref[0, rows, :]
        q = qk[:, :B_KEY_WIDTH] * (B_KEY_DIM ** -0.5)
        k = qk[:, B_KEY_WIDTH:]
        v = v_ref[0, rows, :]
        b_inc = d_all[0:c]
        b_rev = d_all[c:2 * c]
        q_in = (q * jnp.exp(b_inc)).astype(BF16)
        k_st = (k * jnp.exp(b_rev)).astype(BF16)
        dec_last = jnp.exp(b_inc[c - 1:c, :])
        qd = [q.astype(BF16)]
        kd = [k.astype(BF16)]
        for lev in range(nlev):
            dq = d_all[(2 + 2 * lev) * c:(3 + 2 * lev) * c]
            dk = d_all[(3 + 2 * lev) * c:(4 + 2 * lev) * c]
            qd.append((q * jnp.exp(dq)).astype(BF16))
            kd.append((k * jnp.exp(dk)).astype(BF16))
        for h in range(B_HEADS):
            ks = slice(h * B_KEY_DIM, (h + 1) * B_KEY_DIM)
            vs = slice(h * B_VAL_DIM, (h + 1) * B_VAL_DIM)
            att = jnp.zeros((c, c), F32)
            for lev in range(nlev + 1):
                att = att + masks_ref[lev] * _dot_nt(qd[lev][:, ks], kd[lev][:, ks])
            vh = v[:, vs]
            intra = _dot(att.astype(BF16), vh.astype(BF16))
            st_add = _dot(vh.T.astype(BF16), k_st[:, ks])
            st = s_scr[h]
            o_ref[0, rows, vs] = _dot_nt(q_in[:, ks], st.astype(BF16)) + intra
            s_scr[h] = st * dec_last[:, ks] + st_add

    @pl.when(i == pl.num_programs(1) - 1)
    def _():
        so_ref[0] = s_scr[...]


def _gla(qk, v, lx, alpha_up, alpha_b, s0t):
    b, t, _ = qk.shape
    c = min(CHUNK, t)
    nlev = int(round(math.log2(c)))
    mall, masks = _gla_tables(c)
    kw, vw = B_KEY_WIDTH, B_WIDTH
    nc = math.gcd(GLA_CHUNKS_PER_STEP, t // c)
    rows = nc * c
    st_spec = pl.BlockSpec((1, B_HEADS, B_VAL_DIM, B_KEY_DIM), lambda i, j: (i, 0, 0, 0))
    return pl.pallas_call(
        functools.partial(_gla_kernel, c=c, nlev=nlev, nc=nc),
        grid=(b, t // rows),
        in_specs=[pl.BlockSpec((1, rows, 2 * kw), lambda i, j: (i, j, 0)),
                  pl.BlockSpec((1, rows, vw), lambda i, j: (i, j, 0)),
                  pl.BlockSpec((1, rows, XAL_PAD), lambda i, j: (i, j, LORA_PAD // XAL_PAD)),
                  _const_spec((XAL_PAD, kw)), _const_spec((1, kw)),
                  _const_spec(mall.shape), _const_spec(masks.shape), st_spec],
        out_specs=[pl.BlockSpec((1, rows, vw), lambda i, j: (i, j, 0)), st_spec],
        out_shape=[jax.ShapeDtypeStruct((b, t, vw), F32),
                   jax.ShapeDtypeStruct((b, B_HEADS, B_VAL_DIM, B_KEY_DIM), F32)],
        scratch_shapes=[pltpu.VMEM((B_HEADS, B_VAL_DIM, B_KEY_DIM), F32)],
        compiler_params=_cparams(("arbitrary", "arbitrary")),
        name="gla_chunk",
    )(qk, v, lx, alpha_up, alpha_b, jnp.asarray(mall, BF16), jnp.asarray(masks), s0t)


def _post_even_kernel(x_ref, gate_ref, ya_ref, bonus_ref, g_ref, ob_ref, rb_ref, gag_ref, gab_ref, gbg_ref,
                      ind64_ref, ind128_ref, wout_ref, xo_ref):
    ya = ya_ref[0]
    ind64 = ind64_ref[...]
    mu = _seg_sum(ya, ind64) * (1.0 / A_HEAD_DIM)
    dev = ya - mu
    var = _seg_sum(dev * dev, ind64) * (1.0 / A_HEAD_DIM)
    out_a = (dev * lax.rsqrt(var + GN_EPS) * gag_ref[...] + gab_ref[...] + bonus_ref[0]) * g_ref[0]
    ob = ob_ref[0]
    ms = _seg_sum(ob * ob, ind128_ref[...]) * (1.0 / B_VAL_DIM)
    out_b = ob * lax.rsqrt(ms + RMS_EPS) * gbg_ref[...] * _silu(rb_ref[0])
    mix = _dot(out_a.astype(BF16), wout_ref[:A_WIDTH, :]) + _dot(out_b.astype(BF16), wout_ref[A_WIDTH:, :])
    xo_ref[0] = x_ref[0] + gate_ref[0] * mix


def _post_even(x, gate, ya, bonus, g, ob, rb, prm, tm):
    b, t, d = x.shape
    aw = A_WIDTH
    tok = lambda n: pl.BlockSpec((1, tm, n), lambda i, j: (i, j, 0))
    return pl.pallas_call(
        _post_even_kernel,
        grid=(b, t // tm),
        in_specs=[tok(d), pl.BlockSpec((1, 1, d), lambda i, j: (i, 0, 0)), tok(aw), tok(aw), tok(aw), tok(B_WIDTH),
                  tok(B_WIDTH), _const_spec((1, aw)), _const_spec((1, aw)), _const_spec((1, B_WIDTH)),
                  _const_spec((aw, aw)), _const_spec((B_WIDTH, B_WIDTH)), _const_spec((aw + B_WIDTH, d))],
        out_specs=tok(d),
        out_shape=jax.ShapeDtypeStruct((b, t, d), F32),
        compiler_params=_cparams(("arbitrary", "arbitrary")),
        name="even_out",
    )(x, gate, ya, bonus, g, ob, rb, prm['gn_a_g'], prm['gn_a_b'], prm['gn_b_g'], prm['ind64'], prm['ind128'],
      prm['w_out'])


def _fold_rows(x, op):
    return op(x.reshape(x.shape[0] // FOLD_ROWS, FOLD_ROWS, x.shape[1]), axis=0)


def _attn_kernel(x_ref, gate_ref, qt_ref, qit_ref, wit_ref, k_ref, vt_ref, ki_ref, near_ref, tri_ref, wout_ref,
                 xo_ref, key_scr, mask_scr, s_scr, ot_scr, *, past, l_len, topk):
    tq = LANES
    tk = KEY_TILE
    hg = ATTN_HEAD_GROUP
    hd = C_HEAD_DIM
    q0 = past + pl.program_id(1) * tq
    col_pos = q0 + lax.broadcasted_iota(jnp.int32, (1, tq), 1)
    lim = jnp.minimum((col_pos // CHUNK + 1) * CHUNK, l_len)
    n_adm = jnp.minimum(((q0 + tq - 1) // CHUNK + 1) * CHUNK, l_len)
    nt = (n_adm + tk - 1) // tk
    key_neg = _order_key_of(NEG_INF)
    zero_half = jnp.zeros((hd, tq), BF16)

    def head_cols(pair):
        return jnp.concatenate([jnp.concatenate([pair[:hd], zero_half], axis=0),
                                jnp.concatenate([zero_half, pair[hd:]], axis=0)], axis=1)

    w_rows = [wit_ref[0, h:h + 1, :] * (IDX_HEADS ** -0.5 * IDX_DIM ** -0.5) for h in range(IDX_HEADS)]
    qi_pairs = [head_cols(qit_ref[0, p * LANES:(p + 1) * LANES, :]) for p in range(IDX_HEADS // 2)]

    def score_tile(t, c):
        off = pl.multiple_of(t * tk, tk)
        ki = ki_ref[0, pl.ds(off, tk), :]
        score = jnp.zeros((tk, tq), F32)
        for p in range(IDX_HEADS // 2):
            s2 = _dot(ki, qi_pairs[p])
            score = score + jnp.maximum(s2[:, :tq], 0.0) * w_rows[2 * p] + jnp.maximum(s2[:, tq:], 0.0) * w_rows[2 * p + 1]
        kpos = off + lax.broadcasted_iota(jnp.int32, (tk, tq), 0)
        score = jnp.where(score == 0.0, 0.0, score)
        score = jnp.where(kpos < lim, score, NEG_INF)
        bits = lax.bitcast_convert_type(score, jnp.int32)
        key_scr[pl.ds(off, tk), :] = bits ^ ((bits >> 31) & 0x7FFFFFFF)
        return c

    lax.fori_loop(0, nt, score_tile, 0)

    def count_ge(thr):
        def body(t, acc):
            off = pl.multiple_of(t * tk, tk)
            return acc + _fold_rows(jnp.where(key_scr[pl.ds(off, tk), :] >= thr, 1.0, 0.0), jnp.sum)
        acc = lax.fori_loop(0, nt, body, jnp.zeros((FOLD_ROWS, tq), F32))
        return jnp.sum(acc, axis=0, keepdims=True)

    int_min = jnp.full((1, tq), -2 ** 31, jnp.int32)
    thr = jnp.where(count_ge(jnp.zeros((1, tq), jnp.int32)) >= topk, 0, int_min)

    def bit_step(i, thr):
        cand = thr + (jnp.int32(1) << (30 - i))
        return jnp.where(count_ge(cand) >= topk, cand, thr)

    thr = lax.fori_loop(0, 31, bit_step, thr)
    need = topk - count_ge(thr + 1)

    def mask_tile(t, run):
        off = pl.multiple_of(t * tk, tk)
        key = key_scr[pl.ds(off, tk), :]
        eq = key == thr
        prefix = run + _dot(tri_ref[...], jnp.where(eq, 1.0, 0.0).astype(BF16))
        sel = ((key > thr) | (eq & (prefix <= need))) & (key > key_neg)
        mask_scr[pl.ds(off, tk), :] = jnp.where(sel, 0.0, NEG_INF)
        return prefix[tk - 1:tk, :]

    def mask_tile_no_ties(t, c):
        off = pl.multiple_of(t * tk, tk)
        mask_scr[pl.ds(off, tk), :] = jnp.where(key_scr[pl.ds(off, tk), :] >= jnp.maximum(thr, key_neg + 1),
                                                0.0, NEG_INF)
        return c

    tied = (count_ge(thr) > topk) & (thr > key_neg)
    any_tied = jnp.max(jnp.where(tied, 1.0, 0.0)) > 0.0

    @pl.when(any_tied)
    def _():
        lax.fori_loop(0, nt, mask_tile, jnp.zeros((1, tq), F32))

    @pl.when(jnp.logical_not(any_tied))
    def _():
        lax.fori_loop(0, nt, mask_tile_no_ties, 0)

    jq = q0 // LANES

    def group_body(g, c):
        pair_rows = [pl.multiple_of((g * (hg // 2) + pp) * LANES, LANES) for pp in range(hg // 2)]
        q_pairs = [head_cols(qt_ref[0, pl.ds(pr, LANES), :] * (hd ** -0.5)) for pr in pair_rows]

        def logits_tile(t, cc):
            off = pl.multiple_of(t * tk, tk)
            mk = mask_scr[pl.ds(off, tk), :]
            for pp, pr in enumerate(pair_rows):
                s2 = _dot(k_ref[0, pl.ds(off, tk), pl.ds(pr, LANES)], q_pairs[pp])
                s_scr[2 * pp, pl.ds(off, tk), :] = s2[:, :tq] + mk
                s_scr[2 * pp + 1, pl.ds(off, tk), :] = s2[:, tq:] + mk
            return cc

        lax.fori_loop(0, nt, logits_tile, 0)
        for nidx in range(3):
            j = jq - 2 + nidx

            @pl.when(j >= 0)
            def _():
                off = pl.multiple_of(j * LANES, LANES)
                for hh in range(hg):
                    s_scr[hh, pl.ds(off, LANES), :] = s_scr[hh, pl.ds(off, LANES), :] + near_ref[nidx, g * hg + hh]

        def max_tile(t, ms):
            off = pl.multiple_of(t * tk, tk)
            return tuple(jnp.maximum(m, _fold_rows(s_scr[hh, pl.ds(off, tk), :], jnp.max)) for hh, m in enumerate(ms))

        ms = lax.fori_loop(0, nt, max_tile, tuple(jnp.full((FOLD_ROWS, tq), NEG_INF, F32) for _ in range(hg)))
        ms = [jnp.max(m, axis=0, keepdims=True) for m in ms]

        def pv_tile(t, accs):
            off = pl.multiple_of(t * tk, tk)
            new_accs = []
            for hh in range(hg):
                pexp = jnp.exp((s_scr[hh, pl.ds(off, tk), :] - ms[hh]).astype(BF16))
                row0 = pl.multiple_of((g * hg + hh) * VT_HEAD_ROWS, 16)
                new_accs.append(accs[hh] + _dot(vt_ref[0, pl.ds(row0, VT_HEAD_ROWS), pl.ds(off, tk)], pexp))
            return tuple(new_accs)

        accs = lax.fori_loop(0, nt, pv_tile, tuple(jnp.zeros((VT_HEAD_ROWS, tq), F32) for _ in range(hg)))
        for hh in range(hg):
            row0 = pl.multiple_of((g * hg + hh) * hd, hd)
            ot_scr[pl.ds(row0, hd), :] = accs[hh][:hd] / accs[hh][hd:hd + 1]
        return c

    lax.fori_loop(0, C_HEADS // hg, group_body, 0)
    mix = _dot(ot_scr[...].T.astype(BF16), wout_ref[...])
    xo_ref[0] = x_ref[0] + gate_ref[0] * mix


def _order_key_of(value):
    bits = int(np.float32(value).view(np.int32))
    return bits ^ ((bits >> 31) & 0x7FFFFFFF)


def _rel_bucket(rel):
    half = REL_BUCKETS // 2
    max_exact = half // 2
    n = jnp.abs(rel)
    nf = jnp.maximum(n, 1).astype(F32)
    large = max_exact + (jnp.log(nf / max_exact) / math.log(REL_MAX_DIST / max_exact)
                         * (half - max_exact)).astype(jnp.int32)
    large = jnp.minimum(large, half - 1)
    return jnp.where(rel > 0, half, 0) + jnp.where(n < max_exact, n, large)


def _attn(x, gate, qt, qit, wit, k_all, vt_all, ki_all, rel_bias, w_out, past, l_len):
    b, t, d = x.shape
    lp = k_all.shape[1]
    tq = LANES
    topk = min(TOPK_MAX, l_len // 4)
    assert t % tq == 0 and past % LANES == 0 and lp % KEY_TILE == 0 and lp >= l_len
    assert 3 * LANES - (LANES - 1) >= REL_MAX_DIST
    rel = ((jnp.arange(3) - 2) * LANES)[:, None, None] + jnp.arange(LANES)[None, :, None] - jnp.arange(tq)[None, None, :]
    far = rel_bias[_rel_bucket(jnp.int32(-REL_MAX_DIST))]
    near = jnp.transpose(rel_bias[_rel_bucket(rel)] - far, (0, 3, 1, 2)).astype(F32)
    tri = jnp.asarray(np.tril(np.ones((KEY_TILE, KEY_TILE), np.float32)), BF16)
    tok = lambda n: pl.BlockSpec((1, tq, n), lambda i, j: (i, j, 0))
    tok_t = lambda n: pl.BlockSpec((1, n, tq), lambda i, j: (i, 0, j))
    resident = lambda shape: pl.BlockSpec((1,) + shape, lambda i, j: (i, 0, 0), pipeline_mode=pl.Buffered(1))
    return pl.pallas_call(
        functools.partial(_attn_kernel, past=past, l_len=l_len, topk=topk),
        grid=(b, t // tq),
        in_specs=[tok(d), pl.BlockSpec((1, 1, d), lambda i, j: (i, 0, 0)), tok_t(C_WIDTH), tok_t(IDX_HEADS * IDX_DIM),
                  tok_t(IDX_HEADS), resident((lp, C_WIDTH)), resident((C_HEADS * VT_HEAD_ROWS, lp)),
                  resident((lp, LANES)),
                  _const_spec((3, C_HEADS, LANES, tq)), _const_spec((KEY_TILE, KEY_TILE)),
                  _const_spec((C_WIDTH, d))],
        out_specs=tok(d),
        out_shape=jax.ShapeDtypeStruct((b, t, d), F32),
        scratch_shapes=[pltpu.VMEM((lp, tq), jnp.int32), pltpu.VMEM((lp, tq), F32),
                        pltpu.VMEM((ATTN_HEAD_GROUP, lp, tq), F32), pltpu.VMEM((C_WIDTH, tq), F32)],
        compiler_params=_cparams(("arbitrary", "arbitrary")),
        name="dsa_attn",
    )(x, gate, qt, qit, wit, k_all, vt_all, ki_all, near, tri, w_out)


def _prep_params(p):
    d = D_MODEL
    aw = A_WIDTH
    out = {'even': [], 'odd': [], 'ffn': []}
    ind64 = jnp.asarray(np.kron(np.eye(aw // A_HEAD_DIM), np.ones((A_HEAD_DIM, A_HEAD_DIM))), BF16)
    ind128 = jnp.asarray(np.kron(np.eye(B_WIDTH // B_VAL_DIM), np.ones((B_VAL_DIM, B_VAL_DIM))), BF16)
    for j in range(p['w_in_ab'].shape[0]):
        w = p['w_in_ab'][j]
        o = A_COLS
        lora = jnp.pad(w[:, 3 * aw:A_COLS], ((0, 0), (0, LORA_PAD - LORA_W)))
        xal = jnp.pad(w[:, o + 2 * B_KEY_WIDTH + B_WIDTH:o + 2 * B_KEY_WIDTH + B_WIDTH + GLA_RANK],
                      ((0, 0), (0, XAL_PAD - GLA_RANK)))
        w_in = jnp.concatenate([w[:, :3 * aw], w[:, o:o + 2 * B_KEY_WIDTH + B_WIDTH],
                                w[:, o + 2 * B_KEY_WIDTH + B_WIDTH + GLA_RANK:], lora, xal], axis=1).astype(BF16)
        mu = p['mu_a'][j]

        def lora_rows(m, r0):
            return jnp.pad(m, ((r0, LORA_PAD - r0 - m.shape[0]), (0, 0)))

        out['even'].append({
            'w_in': w_in,
            'mu_rkv': mu[None, :3 * aw],
            'mu_l': jnp.pad(mu[None, 3 * aw:], ((0, 0), (0, LORA_PAD - LORA_W))),
            'w0': p['w0'][j][None], 'a0': p['a0'][j][None],
            'wd': lora_rows(p['w_decay_up'][j], 0),
            'wi': lora_rows(p['w_iclr_up'][j], DECAY_LORA),
            'wg': lora_rows(p['w_gate_up'][j], DECAY_LORA + ICLR_LORA),
            'k_k': p['k_k'][j][None], 'k_a': p['k_a'][j][None], 'r_k': p['r_k'][j].reshape(1, aw),
            'gn_a_g': p['gn_a_g'][j][None], 'gn_a_b': p['gn_a_b'][j][None], 'gn_b_g': p['gn_b_g'][j][None],
            'alpha_up': jnp.pad(p['alpha_up'][j], ((0, XAL_PAD - GLA_RANK), (0, 0))),
            'alpha_b': p['alpha_b'][j][None],
            'w_out': p['w_out_ab'][j].astype(BF16),
            'ind64': ind64, 'ind128': ind128,
        })
    for j in range(p['w_in_c'].shape[0]):
        w = p['w_in_c'][j]
        o4 = 3 * C_WIDTH + IDX_HEADS * IDX_DIM
        ki = w[:, o4:o4 + IDX_DIM]
        wi = jnp.pad(w[:, o4 + IDX_DIM:], ((0, 0), (0, LANES - IDX_HEADS)))
        out['odd'].append({
            'w_in': jnp.concatenate([w[:, :o4], ki, ki, wi], axis=1).astype(BF16),
            'w_out': p['w_out_c'][j].astype(BF16),
        })
    for i in range(DEPTH):
        out['ffn'].append({
            'w_up': p['w_ffn_up'][i].astype(BF16), 'conv_w': p['ffn_conv_w'][i], 'conv_b': p['ffn_conv_b'][i][None],
            'w_down': p['w_ffn_down'][i].astype(BF16),
        })
    return out


def _token_tile(t):
    return min(512, t)


def _even_layer(x, mods, gain, shift_buf, wkv0, gla0, prm):
    b, t, d = x.shape
    aw = A_WIDTH
    tm = _token_tile(t)
    sh_m, sc_m, gt_m = mods
    kw2 = 2 * B_KEY_WIDTH
    plain = ((F32, False),)
    cols = [(0, 3 * aw, plain), (3 * aw, kw2, plain), (3 * aw + kw2, B_WIDTH, plain),
            (3 * aw + kw2 + B_WIDTH, B_WIDTH, plain), (3 * aw + kw2 + 2 * B_WIDTH, LORA_PAD + XAL_PAD, plain)]
    rkv, qk_b, v_b, r_b, lx = _proj(x, gain, sh_m, sc_m, prm['w_in'], cols, tm)
    new_shift = jnp.concatenate([rkv[:, t - 1:, :], lx[:, t - 1:, :LORA_W]], axis=-1)
    sb_rkv = shift_buf[:, :, :3 * aw]
    sb_l = jnp.pad(shift_buf[:, :, 3 * aw:], ((0, 0), (0, 0), (0, LORA_PAD - LORA_W)))
    x6, g, bonus = _pre_even(rkv, lx, sb_rkv, sb_l, prm, tm)
    chains = b * A_HEADS
    assert chains == LANES
    x6t = jnp.transpose(x6.reshape(6, b, t, A_HEADS, A_HEAD_DIM), (2, 0, 4, 1, 3)).reshape(t, 6, A_HEAD_DIM, chains)
    s0 = jnp.transpose(wkv0, (2, 3, 0, 1)).reshape(A_HEAD_DIM, A_HEAD_DIM, chains)
    ya_t, s_fin = _rwkv(x6t, s0, min(RWKV_TT, t))
    ya = jnp.transpose(ya_t.reshape(t, A_HEAD_DIM, b, A_HEADS), (2, 0, 3, 1)).reshape(b, t, aw)
    wkv_new = jnp.transpose(s_fin.reshape(A_HEAD_DIM, A_HEAD_DIM, b, A_HEADS), (2, 3, 0, 1))
    ob, gla_t = _gla(qk_b, v_b, lx, prm['alpha_up'], prm['alpha_b'], jnp.swapaxes(gla0, 2, 3))
    gla_new = jnp.swapaxes(gla_t, 2, 3)
    x = _post_even(x, gt_m, ya, bonus, g, ob, r_b, prm, tm)
    return x, new_shift, wkv_new, gla_new


def _odd_layer(x, mods, gain, k_cache, v_cache, ki_cache, rel_bias, prm):
    b, t, d = x.shape
    tp = -(-t // LANES) * LANES
    xp = jnp.pad(x, ((0, 0), (0, tp - t), (0, 0)))
    tm = _token_tile(tp)
    sh_m, sc_m, gt_m = mods
    cw = C_WIDTH
    iw = IDX_HEADS * IDX_DIM
    cols = [(0, cw, ((BF16, True),)),
            (cw, cw, ((F32, False), (BF16, False))),
            (2 * cw, cw, ((F32, False), (BF16, True))),
            (3 * cw, iw, ((BF16, True),)),
            (3 * cw + iw, LANES, ((F32, False), (BF16, False))),
            (3 * cw + iw + LANES, LANES, ((F32, True),))]
    qt, k, k_bf, v, vt_bf, qit, kiki, kiki_bf, wit = _proj(xp, gain, sh_m, sc_m, prm['w_in'], cols, tm)
    past = k_cache.shape[1]
    l_len = past + t
    lp = -(-l_len // KEY_TILE) * KEY_TILE
    kc = k_cache.reshape(b, past, cw).astype(BF16)
    vct = jnp.swapaxes(v_cache.reshape(b, past, cw).astype(BF16), 1, 2)
    kic = jnp.concatenate([ki_cache, ki_cache], axis=-1).astype(BF16)
    pad = ((0, 0), (0, lp - l_len), (0, 0))
    k_all = jnp.pad(jnp.concatenate([kc, k_bf[:, :t]], axis=1), pad)
    vt_heads = jnp.concatenate([vct, vt_bf[:, :, :t]], axis=2).reshape(b, C_HEADS, C_HEAD_DIM, l_len)
    extra = jnp.zeros((b, C_HEADS, VT_HEAD_ROWS - C_HEAD_DIM, l_len), BF16).at[:, :, 0].set(1.0)
    vt_all = jnp.concatenate([vt_heads, extra], axis=2).reshape(b, C_HEADS * VT_HEAD_ROWS, l_len)
    vt_all = jnp.pad(vt_all, ((0, 0), (0, 0), (0, lp - l_len)))
    ki_all = jnp.pad(jnp.concatenate([kic, kiki_bf[:, :t]], axis=1), pad)
    xo = _attn(xp, gt_m, qt, qit, wit, k_all, vt_all, ki_all, rel_bias, prm['w_out'], past, l_len)
    return (xo[:, :t], k[:, :t].reshape(b, t, C_HEADS, C_HEAD_DIM), v[:, :t].reshape(b, t, C_HEADS, C_HEAD_DIM),
            kiki[:, :t, :IDX_DIM])


def _run_group(x, mod, shift_bufs, wkv_states, gla_states, k_caches, v_caches, ki_caches, ffn_bufs, raw, prm):
    b, t, d = x.shape
    tm = _token_tile(t)
    shifts, wkvs, glas, ks, vs, kis, ffns = [], [], [], [], [], [], []
    for i in range(DEPTH):
        m6 = [mod[i][:, None, n * d:(n + 1) * d] for n in range(N_MOD)]
        j = i // 2
        gain_mix = raw['norm_mix'][i][None]
        if i % 2 == 0:
            x, s_new, wkv_new, gla_new = _even_layer(x, m6[:3], gain_mix, shift_bufs[j], wkv_states[j],
                                                     gla_states[j], prm['even'][j])
            shifts.append(s_new)
            wkvs.append(wkv_new)
            glas.append(gla_new)
        else:
            x, k_new, v_new, ki_new = _odd_layer(x, m6[:3], gain_mix, k_caches[j], v_caches[j], ki_caches[j],
                                                 raw['rel_bias'], prm['odd'][j])
            ks.append(k_new)
            vs.append(v_new)
            kis.append(ki_new)
        f = prm['ffn'][i]
        x, f_buf = _ffn(x, raw['norm_ffn'][i][None], m6[3], m6[4], m6[5], ffn_bufs[i], f['w_up'], f['conv_w'],
                        f['conv_b'], f['w_down'], raw['norm_final'][None], min(FFN_TOKEN_TILE, t),
                        final=(i == DEPTH - 1))
        ffns.append(f_buf)
    return (x, jnp.stack(shifts), jnp.stack(wkvs), jnp.stack(glas), jnp.stack(ks), jnp.stack(vs),
            jnp.stack(kis), jnp.stack(ffns))


def kernel(x_prompt, x_sample, c_prompt, c_sample, state_shift_ab, state_wkv, state_gla, cache_k, cache_v, cache_kidx, state_ffn_conv, w_ada, b_ada, norm_mix, norm_ffn, norm_final, w_in_ab, mu_a, w0, w_decay_up, a0, w_iclr_up, w_gate_up, k_k, k_a, r_k, gn_a_g, gn_a_b, alpha_up, alpha_b, gn_b_g, w_out_ab, w_in_c, rel_bias, w_out_c, w_ffn_up, ffn_conv_w, ffn_conv_b, w_ffn_down):
    raw = {
        'norm_mix': norm_mix, 'norm_ffn': norm_ffn, 'norm_final': norm_final, 'rel_bias': rel_bias,
        'w_in_ab': w_in_ab, 'mu_a': mu_a, 'w0': w0, 'w_decay_up': w_decay_up, 'a0': a0, 'w_iclr_up': w_iclr_up,
        'w_gate_up': w_gate_up, 'k_k': k_k, 'k_a': k_a, 'r_k': r_k, 'gn_a_g': gn_a_g, 'gn_a_b': gn_a_b,
        'alpha_up': alpha_up, 'alpha_b': alpha_b, 'gn_b_g': gn_b_g, 'w_out_ab': w_out_ab, 'w_in_c': w_in_c,
        'w_out_c': w_out_c, 'w_ffn_up': w_ffn_up, 'ffn_conv_w': ffn_conv_w, 'ffn_conv_b': ffn_conv_b,
        'w_ffn_down': w_ffn_down,
    }
    prm = _prep_params(raw)
    n_even = w_in_ab.shape[0]
    n_odd = w_in_c.shape[0]
    dt = x_prompt.dtype
    bp = x_prompt.shape[0]
    mod = _ada(jnp.concatenate([c_prompt, c_sample], axis=0), w_ada, b_ada)
    mod_p, mod_s = mod[:, :bp], mod[:, bp:]
    out_p = _run_group(
        x_prompt, mod_p,
        jnp.zeros((n_even, bp, 1, A_COLS), dt),
        jnp.zeros((n_even, bp, A_HEADS, A_HEAD_DIM, A_HEAD_DIM), dt),
        jnp.zeros((n_even, bp, B_HEADS, B_KEY_DIM, B_VAL_DIM), dt),
        jnp.zeros((n_odd, bp, 0, C_HEADS, C_HEAD_DIM), dt),
        jnp.zeros((n_odd, bp, 0, C_HEADS, C_HEAD_DIM), dt),
        jnp.zeros((n_odd, bp, 0, IDX_DIM), dt),
        jnp.zeros((DEPTH, bp, CONV_W - 1, 2 * D_FF), dt),
        raw, prm)
    out_s = _run_group(x_sample, mod_s, state_shift_ab, state_wkv, state_gla, cache_k, cache_v, cache_kidx,
                       state_ffn_conv, raw, prm)
    res = []
    for a, b in zip(out_p, out_s):
        res.extend([a, b])
    return tuple(res)
```

```python
import functools
import math

import jax
import jax.numpy as jnp
import numpy as np
from jax import lax
from jax.experimental import pallas as pl
from jax.experimental.pallas import tpu as pltpu

F32 = jnp.float32
BF16 = jnp.bfloat16
HI = lax.Precision.HIGHEST

D_MODEL = 1024
DEPTH = 4
CHUNK = 64
A_HEADS = 8
A_HEAD_DIM = 64
A_WIDTH = A_HEADS * A_HEAD_DIM
DECAY_LORA = 32
ICLR_LORA = 32
GATE_LORA = 96
LORA_W = DECAY_LORA + ICLR_LORA + GATE_LORA
A_COLS = 3 * A_WIDTH + LORA_W
B_HEADS = 4
B_KEY_DIM = 64
B_VAL_DIM = 128
B_KEY_WIDTH = B_HEADS * B_KEY_DIM
B_WIDTH = B_HEADS * B_VAL_DIM
GLA_RANK = 16
GLA_TAU = 16.0
C_HEADS = 16
C_HEAD_DIM = 64
C_WIDTH = C_HEADS * C_HEAD_DIM
IDX_HEADS = 8
IDX_DIM = 64
TOPK_MAX = 256
QUERY_BLOCK = 128
REL_BUCKETS = 32
REL_MAX_DIST = 256
D_FF = 2816
CONV_W = 3
N_MOD = 6
RMS_EPS = 1e-6
GN_EPS = 64e-5
NEG_INF = -1e30

LANES = 128
SUBLANES = 8
VMEM_LIMIT = 56 * 1024 * 1024

LORA_PAD = 256
XAL_PAD = 128
KEY_TILE = 512
ATTN_HEAD_GROUP = 8
FOLD_ROWS = 64
VT_HEAD_ROWS = C_HEAD_DIM + 16
FFN_CHUNK = 256
FFN_TOKEN_TILE = 1024
RWKV_TT = 32
GLA_CHUNKS_PER_STEP = 4


def _cparams(sem):
    return pltpu.CompilerParams(dimension_semantics=sem, vmem_limit_bytes=VMEM_LIMIT)


def _const_spec(shape):
    nd = len(shape)
    return pl.BlockSpec(shape, lambda *_: (0,) * nd, pipeline_mode=pl.Buffered(1))


def _sigmoid(x):
    return 1.0 / (1.0 + jnp.exp(-x))


def _silu(x):
    return x * _sigmoid(x)


def _softplus(x):
    return jnp.maximum(x, 0.0) + jnp.log(1.0 + jnp.exp(-jnp.abs(x)))


def _dot(a, b, precision=None):
    return jnp.dot(a, b, preferred_element_type=F32, precision=precision)


def _dot_nt(a, b):
    return lax.dot_general(a, b, (((1,), (1,)), ((), ())), preferred_element_type=F32)


def _seg_sum(x, ind_bf16):
    hi = x.astype(BF16)
    lo = (x - hi.astype(F32)).astype(BF16)
    return _dot(hi, ind_bf16) + _dot(lo, ind_bf16)


def _modulated_norm(x, gain, shift, scale):
    var = jnp.mean(x * x, axis=-1, keepdims=True)
    h = x * lax.rsqrt(var + RMS_EPS) * gain
    return h * (1.0 + scale) + shift


def _ada_kernel(c_ref, w_ref, b_ref, o_ref):
    c = c_ref[...]
    o_ref[0] = _dot(_silu(c), w_ref[0], HI) + b_ref[0]


def _ada(c, w_ada, b_ada):
    m, d = c.shape
    depth, _, n = w_ada.shape
    tn = 1536
    return pl.pallas_call(
        _ada_kernel,
        grid=(depth, n // tn),
        in_specs=[pl.BlockSpec((m, d), lambda l, j: (0, 0)),
                  pl.BlockSpec((1, d, tn), lambda l, j: (l, 0, j)),
                  pl.BlockSpec((1, 1, tn), lambda l, j: (l, 0, j))],
        out_specs=pl.BlockSpec((1, m, tn), lambda l, j: (l, 0, j)),
        out_shape=jax.ShapeDtypeStruct((depth, m, n), F32),
        compiler_params=_cparams(("arbitrary", "arbitrary")),
        name="ada_mod",
    )(c, w_ada, b_ada.reshape(depth, 1, n))


def _proj_kernel(x_ref, gain_ref, shift_ref, scale_ref, w_ref, *out_refs, groups):
    h = _modulated_norm(x_ref[0], gain_ref[...], shift_ref[0], scale_ref[0]).astype(BF16)
    refs = iter(out_refs)
    for c0, n, forms in groups:
        y = _dot(h, w_ref[:, c0:c0 + n])
        for dt, transposed in forms:
            next(refs)[0] = (y.T if transposed else y).astype(dt)


def _proj(x, gain, shift, scale, w, groups, tm):
    b, t, d = x.shape
    n_all = w.shape[1]
    mod_spec = pl.BlockSpec((1, 1, d), lambda i, j: (i, 0, 0))
    outs = [(n, dt, tr) for _, n, forms in groups for dt, tr in forms]
    return pl.pallas_call(
        functools.partial(_proj_kernel, groups=groups),
        grid=(b, t // tm),
        in_specs=[pl.BlockSpec((1, tm, d), lambda i, j: (i, j, 0)),
                  _const_spec((1, d)), mod_spec, mod_spec,
                  _const_spec((d, n_all))],
        out_specs=[pl.BlockSpec((1, n, tm), lambda i, j: (i, 0, j)) if tr else
                   pl.BlockSpec((1, tm, n), lambda i, j: (i, j, 0)) for n, _, tr in outs],
        out_shape=[jax.ShapeDtypeStruct((b, n, t) if tr else (b, t, n), dt) for n, dt, tr in outs],
        compiler_params=_cparams(("arbitrary", "arbitrary")),
        name="norm_proj",
    )(x, gain, shift, scale, w)


def _ffn_kernel(x_ref, gain_ref, shift_ref, scale_ref, gate_ref, buf_ref, wup_ref, cw_ref, cb_ref, wdn_ref,
                fgain_ref, xo_ref, bufo_ref, carry_ref, *, tm, final):
    t = pl.program_id(1)

    @pl.when(t == 0)
    def _():
        carry_ref[...] = buf_ref[0]

    x = x_ref[0]
    h = _modulated_norm(x, gain_ref[...], shift_ref[0], scale_ref[0]).astype(BF16)
    row = lax.broadcasted_iota(jnp.int32, (tm, FFN_CHUNK), 0)

    def conv(u, c0):
        c_old = carry_ref[0:1, c0:c0 + FFN_CHUNK]
        c_new = carry_ref[1:2, c0:c0 + FFN_CHUNK]
        prev1 = jnp.where(row == 0, c_new, pltpu.roll(u, 1, 0))
        prev2 = jnp.where(row == 0, c_old, jnp.where(row == 1, c_new, pltpu.roll(u, 2, 0)))
        carry_ref[:, c0:c0 + FFN_CHUNK] = u[tm - 2:tm, :]
        return (cb_ref[:, c0:c0 + FFN_CHUNK] + cw_ref[0:1, c0:c0 + FFN_CHUNK] * prev2
                + cw_ref[1:2, c0:c0 + FFN_CHUNK] * prev1 + cw_ref[2:3, c0:c0 + FFN_CHUNK] * u)

    acc = jnp.zeros((tm, D_MODEL), F32)
    for c in range(D_FF // FFN_CHUNK):
        g0 = c * FFN_CHUNK
        v0 = D_FF + g0
        gate = conv(_dot(h, wup_ref[:, g0:g0 + FFN_CHUNK]), g0)
        val = conv(_dot(h, wup_ref[:, v0:v0 + FFN_CHUNK]), v0)
        act = (_silu(gate) * val).astype(BF16)
        acc = acc + _dot(act, wdn_ref[g0:g0 + FFN_CHUNK, :])

    y = x + gate_ref[0] * acc
    if final:
        y = y * lax.rsqrt(jnp.mean(y * y, axis=-1, keepdims=True) + RMS_EPS) * fgain_ref[...]
    xo_ref[0] = y

    @pl.when(t == pl.num_programs(1) - 1)
    def _():
        bufo_ref[0] = carry_ref[...]


def _ffn(x, gain, shift, scale, gate, buf, w_up, conv_w, conv_b, w_down, final_gain, tm, final):
    b, t, d = x.shape
    f2 = 2 * D_FF
    mod_spec = pl.BlockSpec((1, 1, d), lambda i, j: (i, 0, 0))
    buf_spec = pl.BlockSpec((1, CONV_W - 1, f2), lambda i, j: (i, 0, 0))
    return pl.pallas_call(
        functools.partial(_ffn_kernel, tm=tm, final=final),
        grid=(b, t // tm),
        in_specs=[pl.BlockSpec((1, tm, d), lambda i, j: (i, j, 0)),
                  _const_spec((1, d)), mod_spec, mod_spec, mod_spec, buf_spec,
                  _const_spec((d, f2)), _const_spec((CONV_W, f2)), _const_spec((1, f2)),
                  _const_spec((D_FF, d)), _const_spec((1, d))],
        out_specs=[pl.BlockSpec((1, tm, d), lambda i, j: (i, j, 0)), buf_spec],
        out_shape=[jax.ShapeDtypeStruct((b, t, d), F32),
                   jax.ShapeDtypeStruct((b, CONV_W - 1, f2), F32)],
        scratch_shapes=[pltpu.VMEM((CONV_W - 1, f2), F32)],
        compiler_params=_cparams(("arbitrary", "arbitrary")),
        name="conv_ffn",
    )(x, gain, shift, scale, gate, buf, w_up, conv_w, conv_b, w_down, final_gain)


def _pre_even_kernel(rkv_ref, lx_ref, sb_rkv_ref, sb_l_ref, mu_rkv_ref, mu_l_ref, w0_ref, wd_ref, a0_ref, wi_ref,
                     wg_ref, kk_ref, ka_ref, rk_ref, ind_ref, x6_ref, g_ref, bonus_ref, c_rkv, c_l, *, tm):
    t = pl.program_id(1)

    @pl.when(t == 0)
    def _():
        c_rkv[...] = sb_rkv_ref[0]
        c_l[...] = sb_l_ref[0]

    def shifted(p, carry_ref, mu):
        row = lax.broadcasted_iota(jnp.int32, p.shape, 0)
        prev = jnp.where(row == 0, carry_ref[...], pltpu.roll(p, 1, 0))
        carry_ref[...] = p[tm - 1:tm, :]
        return p + (prev - p) * mu

    pa = shifted(rkv_ref[0], c_rkv, mu_rkv_ref[...])
    lora = shifted(lx_ref[0][:, :LORA_PAD], c_l, mu_l_ref[...])
    r = pa[:, :A_WIDTH]
    k = pa[:, A_WIDTH:2 * A_WIDTH]
    v = pa[:, 2 * A_WIDTH:]
    w_log = -_softplus(-(w0_ref[...] + _dot(jnp.tanh(lora), wd_ref[...], HI))) - 0.5
    decay = jnp.exp(-jnp.exp(w_log))
    a = _sigmoid(a0_ref[...] + _dot(lora, wi_ref[...], HI))
    g = _dot(_sigmoid(lora), wg_ref[...], HI)
    kk = k * kk_ref[...]
    k2 = k * (1.0 + (a - 1.0) * ka_ref[...])
    ind = ind_ref[...]
    kk = kk * lax.rsqrt(jnp.maximum(_seg_sum(kk * kk, ind), 1e-12))
    bonus = _seg_sum(r * k2 * rk_ref[...], ind) * v
    x6_ref[0] = r
    x6_ref[1] = decay
    x6_ref[2] = k2
    x6_ref[3] = v
    x6_ref[4] = kk
    x6_ref[5] = kk * a
    g_ref[0] = g
    bonus_ref[0] = bonus


def _pre_even(rkv, lx, sb_rkv, sb_l, prm, tm):
    b, t, _ = rkv.shape
    aw = A_WIDTH
    row = lambda n: _const_spec((1, n))
    return pl.pallas_call(
        functools.partial(_pre_even_kernel, tm=tm),
        grid=(b, t // tm),
        in_specs=[pl.BlockSpec((1, tm, 3 * aw), lambda i, j: (i, j, 0)),
                  pl.BlockSpec((1, tm, LORA_PAD + XAL_PAD), lambda i, j: (i, j, 0)),
                  pl.BlockSpec((1, 1, 3 * aw), lambda i, j: (i, 0, 0)),
                  pl.BlockSpec((1, 1, LORA_PAD), lambda i, j: (i, 0, 0)),
                  row(3 * aw), row(LORA_PAD), row(aw), _const_spec((LORA_PAD, aw)), row(aw),
                  _const_spec((LORA_PAD, aw)), _const_spec((LORA_PAD, aw)), row(aw), row(aw), row(aw),
                  _const_spec((aw, aw))],
        out_specs=[pl.BlockSpec((6, tm, aw), lambda i, j: (0, j, i)),
                   pl.BlockSpec((1, tm, aw), lambda i, j: (i, j, 0)),
                   pl.BlockSpec((1, tm, aw), lambda i, j: (i, j, 0))],
        out_shape=[jax.ShapeDtypeStruct((6, t, b * aw), F32),
                   jax.ShapeDtypeStruct((b, t, aw), F32),
                   jax.ShapeDtypeStruct((b, t, aw), F32)],
        scratch_shapes=[pltpu.VMEM((1, 3 * aw), F32), pltpu.VMEM((1, LORA_PAD), F32)],
        compiler_params=_cparams(("arbitrary", "arbitrary")),
        name="rwkv_prep",
    )(rkv, lx, sb_rkv, sb_l, prm['mu_rkv'], prm['mu_l'], prm['w0'], prm['wd'], prm['a0'], prm['wi'], prm['wg'],
      prm['k_k'], prm['k_a'], prm['r_k'], prm['ind64'])


def _rwkv_kernel(x_ref, s0_ref, y_ref, so_ref, s_scr, *, tt):
    n = A_HEAD_DIM
    g = n // SUBLANES
    i = pl.program_id(0)

    @pl.when(i == 0)
    def _():
        s_scr[...] = s0_ref[...]

    def allsum(x):
        x = x + pltpu.roll(x, 4, 0)
        x = x + pltpu.roll(x, 2, 0)
        return x + pltpu.roll(x, 1, 0)

    def ksum(x3):
        return allsum(jnp.sum(x3, axis=0))

    def step(t, carry):
        def ld(j):
            return x_ref[j, t].reshape(g, SUBLANES, LANES)
        r, w, k, kk, b = ld(0), ld(1), ld(2), ld(4), ld(5)
        wr = w * r
        br = ksum(b * r)
        kr = ksum(k * r)

        def vbody(v, c):
            sv = s_scr[v].reshape(g, SUBLANES, LANES)
            sa = ksum(sv * kk)
            z = ksum(sv * wr)
            vv = jnp.broadcast_to(x_ref[3, t, pl.ds(v, 1), :], (SUBLANES, LANES))
            s_scr[v] = (sv * w - sa[None] * b + vv[None] * k).reshape(n, LANES)
            y = z - sa * br + vv * kr
            y_ref[t, pl.ds(v, 1), :] = y[0:1]
            return c

        lax.fori_loop(0, n, vbody, 0, unroll=8)
        return carry

    lax.fori_loop(0, tt, step, 0)

    @pl.when(i == pl.num_programs(0) - 1)
    def _():
        so_ref[...] = s_scr[...]


def _rwkv(x6t, s0, tt):
    t = x6t.shape[1]
    n = A_HEAD_DIM
    return pl.pallas_call(
        functools.partial(_rwkv_kernel, tt=tt),
        grid=(t // tt,),
        in_specs=[pl.BlockSpec((6, tt, n, LANES), lambda i: (0, i, 0, 0)),
                  pl.BlockSpec((n, n, LANES), lambda i: (0, 0, 0))],
        out_specs=[pl.BlockSpec((tt, n, LANES), lambda i: (i, 0, 0)),
                   pl.BlockSpec((n, n, LANES), lambda i: (0, 0, 0))],
        out_shape=[jax.ShapeDtypeStruct((t, n, LANES), F32),
                   jax.ShapeDtypeStruct((n, n, LANES), F32)],
        scratch_shapes=[pltpu.VMEM((n, n, LANES), F32)],
        compiler_params=_cparams(("arbitrary",)),
        name="rwkv_scan",
    )(x6t, s0)


def _gla_tables(c):
    idx = np.arange(c)
    t = idx[:, None]
    r = idx[None, :]
    mats = [(r <= t), (r > t)]
    masks = [np.eye(c, dtype=bool)]
    m = c // 2
    while m >= 1:
        mid = (idx // (2 * m)) * 2 * m + m - 1
        right = (idx % (2 * m)) >= m
        mats.append(right[:, None] & (r > mid[:, None]) & (r <= t))
        mats.append((~right)[:, None] & (r > t) & (r <= mid[:, None]))
        masks.append(((idx // (2 * m))[:, None] == (idx // (2 * m))[None, :]) & right[:, None] & (~right)[None, :])
        m //= 2
    return (np.concatenate(mats, 0).astype(np.float32), np.stack(masks).astype(np.float32))


def _gla_kernel(qk_ref, v_ref, lx_ref, au_ref, ab_ref, mall_ref, masks_ref, s0_ref, o_ref, so_ref, s_scr, *, c, nlev,
                nc):
    i = pl.program_id(1)

    @pl.when(i == 0)
    def _():
        s_scr[...] = s0_ref[0]

    z_all = _dot(lx_ref[0], au_ref[...], HI) + ab_ref[...]
    g_all = -_softplus(-z_all) * (1.0 / GLA_TAU)
    mall = mall_ref[...]
    for j in range(nc):
        rows = slice(j * c, (j + 1) * c)
        g = g_all[rows]
        g_hi = g.astype(BF16)
        g_mid = (g - g_hi.astype(F32)).astype(BF16)
        g_lo = (g - g_hi.astype(F32) - g_mid.astype(F32)).astype(BF16)
        d_all = _dot(mall, g_hi) + _dot(mall, g_mid) + _dot(mall, g_lo)
        qk = qk_ref[0, rows, :]
        q = qk[:, :B_KEY_WIDTH] * (B_KEY_DIM ** -0.5)
        k = qk[:, B_KEY_WIDTH:]
        v = v_ref[0, rows, :]
        b_inc = d_all[0:c]
        b_rev = d_all[c:2 * c]
        q_in = (q * jnp.exp(b_inc)).astype(BF16)
        k_st = (k * jnp.exp(b_rev)).astype(BF16)
        dec_last = jnp.exp(b_inc[c - 1:c, :])
        qd = [q.astype(BF16)]
        kd = [k.astype(BF16)]
        for lev in range(nlev):
            dq = d_all[(2 + 2 * lev) * c:(3 + 2 * lev) * c]
            dk = d_all[(3 + 2 * lev) * c:(4 + 2 * lev) * c]
            qd.append((q * jnp.exp(dq)).astype(BF16))
            kd.append((k * jnp.exp(dk)).astype(BF16))
        for h in range(B_HEADS):
            ks = slice(h * B_KEY_DIM, (h + 1) * B_KEY_DIM)
            vs = slice(h * B_VAL_DIM, (h + 1) * B_VAL_DIM)
            att = jnp.zeros((c, c), F32)
            for lev in range(nlev + 1):
                att = att + masks_ref[lev] * _dot_nt(qd[lev][:, ks], kd[lev][:, ks])
            vh = v[:, vs]
            intra = _dot(att.astype(BF16), vh.astype(BF16))
            st_add = _dot(vh.T.astype(BF16), k_st[:, ks])
            st = s_scr[h]
            o_ref[0, rows, vs] = _dot_nt(q_in[:, ks], st.astype(BF16)) + intra
            s_scr[h] = st * dec_last[:, ks] + st_add

    @pl.when(i == pl.num_programs(1) - 1)
    def _():
        so_ref[0] = s_scr[...]


def _gla(qk, v, lx, alpha_up, alpha_b, s0t):
    b, t, _ = qk.shape
    c = min(CHUNK, t)
    nlev = int(round(math.log2(c)))
    mall, masks = _gla_tables(c)
    kw, vw = B_KEY_WIDTH, B_WIDTH
    nc = math.gcd(GLA_CHUNKS_PER_STEP, t // c)
    rows = nc * c
    st_spec = pl.BlockSpec((1, B_HEADS, B_VAL_DIM, B_KEY_DIM), lambda i, j: (i, 0, 0, 0))
    return pl.pallas_call(
        functools.partial(_gla_kernel, c=c, nlev=nlev, nc=nc),
        grid=(b, t // rows),
        in_specs=[pl.BlockSpec((1, rows, 2 * kw), lambda i, j: (i, j, 0)),
                  pl.BlockSpec((1, rows, vw), lambda i, j: (i, j, 0)),
                  pl.BlockSpec((1, rows, XAL_PAD), lambda i, j: (i, j, LORA_PAD // XAL_PAD)),
                  _const_spec((XAL_PAD, kw)), _const_spec((1, kw)),
                  _const_spec(mall.shape), _const_spec(masks.shape), st_spec],
        out_specs=[pl.BlockSpec((1, rows, vw), lambda i, j: (i, j, 0)), st_spec],
        out_shape=[jax.ShapeDtypeStruct((b, t, vw), F32),
                   jax.ShapeDtypeStruct((b, B_HEADS, B_VAL_DIM, B_KEY_DIM), F32)],
        scratch_shapes=[pltpu.VMEM((B_HEADS, B_VAL_DIM, B_KEY_DIM), F32)],
        compiler_params=_cparams(("arbitrary", "arbitrary")),
        name="gla_chunk",
    )(qk, v, lx, alpha_up, alpha_b, jnp.asarray(mall, BF16), jnp.asarray(masks), s0t)


def _post_even_kernel(x_ref, gate_ref, ya_ref, bonus_ref, g_ref, ob_ref, rb_ref, gag_ref, gab_ref, gbg_ref,
                      ind64_ref, ind128_ref, wout_ref, xo_ref):
    ya = ya_ref[0]
    ind64 = ind64_ref[...]
    mu = _seg_sum(ya, ind64) * (1.0 / A_HEAD_DIM)
    dev = ya - mu
    var = _seg_sum(dev * dev, ind64) * (1.0 / A_HEAD_DIM)
    out_a = (dev * lax.rsqrt(var + GN_EPS) * gag_ref[...] + gab_ref[...] + bonus_ref[0]) * g_ref[0]
    ob = ob_ref[0]
    ms = _seg_sum(ob * ob, ind128_ref[...]) * (1.0 / B_VAL_DIM)
    out_b = ob * lax.rsqrt(ms + RMS_EPS) * gbg_ref[...] * _silu(rb_ref[0])
    mix = _dot(out_a.astype(BF16), wout_ref[:A_WIDTH, :]) + _dot(out_b.astype(BF16), wout_ref[A_WIDTH:, :])
    xo_ref[0] = x_ref[0] + gate_ref[0] * mix


def _post_even(x, gate, ya, bonus, g, ob, rb, prm, tm):
    b, t, d = x.shape
    aw = A_WIDTH
    tok = lambda n: pl.BlockSpec((1, tm, n), lambda i, j: (i, j, 0))
    return pl.pallas_call(
        _post_even_kernel,
        grid=(b, t // tm),
        in_specs=[tok(d), pl.BlockSpec((1, 1, d), lambda i, j: (i, 0, 0)), tok(aw), tok(aw), tok(aw), tok(B_WIDTH),
                  tok(B_WIDTH), _const_spec((1, aw)), _const_spec((1, aw)), _const_spec((1, B_WIDTH)),
                  _const_spec((aw, aw)), _const_spec((B_WIDTH, B_WIDTH)), _const_spec((aw + B_WIDTH, d))],
        out_specs=tok(d),
        out_shape=jax.ShapeDtypeStruct((b, t, d), F32),
        compiler_params=_cparams(("arbitrary", "arbitrary")),
        name="even_out",
    )(x, gate, ya, bonus, g, ob, rb, prm['gn_a_g'], prm['gn_a_b'], prm['gn_b_g'], prm['ind64'], prm['ind128'],
      prm['w_out'])


def _fold_rows(x, op):
    return op(x.reshape(x.shape[0] // FOLD_ROWS, FOLD_ROWS, x.shape[1]), axis=0)


def _attn_kernel(x_ref, gate_ref, qt_ref, qit_ref, wit_ref, k_ref, vt_ref, ki_ref, near_ref, tri_ref, wout_ref,
                 xo_ref, key_scr, mask_scr, s_scr, ot_scr, *, past, l_len, topk):
    tq = LANES
    tk = KEY_TILE
    hg = ATTN_HEAD_GROUP
    hd = C_HEAD_DIM
    q0 = past + pl.program_id(1) * tq
    col_pos = q0 + lax.broadcasted_iota(jnp.int32, (1, tq), 1)
    lim = jnp.minimum((col_pos // CHUNK + 1) * CHUNK, l_len)
    n_adm = jnp.minimum(((q0 + tq - 1) // CHUNK + 1) * CHUNK, l_len)
    nt = (n_adm + tk - 1) // tk
    key_neg = _order_key_of(NEG_INF)
    zero_half = jnp.zeros((hd, tq), BF16)

    def head_cols(pair):
        return jnp.concatenate([jnp.concatenate([pair[:hd], zero_half], axis=0),
                                jnp.concatenate([zero_half, pair[hd:]], axis=0)], axis=1)

    w_rows = [wit_ref[0, h:h + 1, :] * (IDX_HEADS ** -0.5 * IDX_DIM ** -0.5) for h in range(IDX_HEADS)]
    qi_pairs = [head_cols(qit_ref[0, p * LANES:(p + 1) * LANES, :]) for p in range(IDX_HEADS // 2)]

    def score_tile(t, c):
        off = pl.multiple_of(t * tk, tk)
        ki = ki_ref[0, pl.ds(off, tk), :]
        score = jnp.zeros((tk, tq), F32)
        for p in range(IDX_HEADS // 2):
            s2 = _dot(ki, qi_pairs[p])
            score = score + jnp.maximum(s2[:, :tq], 0.0) * w_rows[2 * p] + jnp.maximum(s2[:, tq:], 0.0) * w_rows[2 * p + 1]
        kpos = off + lax.broadcasted_iota(jnp.int32, (tk, tq), 0)
        score = jnp.where(score == 0.0, 0.0, score)
        score = jnp.where(kpos < lim, score, NEG_INF)
        bits = lax.bitcast_convert_type(score, jnp.int32)
        key_scr[pl.ds(off, tk), :] = bits ^ ((bits >> 31) & 0x7FFFFFFF)
        return c

    lax.fori_loop(0, nt, score_tile, 0)

    def count_ge(thr):
        def body(t, acc):
            off = pl.multiple_of(t * tk, tk)
            return acc + _fold_rows(jnp.where(key_scr[pl.ds(off, tk), :] >= thr, 1.0, 0.0), jnp.sum)
        acc = lax.fori_loop(0, nt, body, jnp.zeros((FOLD_ROWS, tq), F32))
        return jnp.sum(acc, axis=0, keepdims=True)

    int_min = jnp.full((1, tq), -2 ** 31, jnp.int32)
    thr = jnp.where(count_ge(jnp.zeros((1, tq), jnp.int32)) >= topk, 0, int_min)

    def bit_step(i, thr):
        cand = thr + (jnp.int32(1) << (30 - i))
        return jnp.where(count_ge(cand) >= topk, cand, thr)

    thr = lax.fori_loop(0, 31, bit_step, thr)
    need = topk - count_ge(thr + 1)

    def mask_tile(t, run):
        off = pl.multiple_of(t * tk, tk)
        key = key_scr[pl.ds(off, tk), :]
        eq = key == thr
        prefix = run + _dot(tri_ref[...], jnp.where(eq, 1.0, 0.0).astype(BF16))
        sel = ((key > thr) | (eq & (prefix <= need))) & (key > key_neg)
        mask_scr[pl.ds(off, tk), :] = jnp.where(sel, 0.0, NEG_INF)
        return prefix[tk - 1:tk, :]

    def mask_tile_no_ties(t, c):
        off = pl.multiple_of(t * tk, tk)
        mask_scr[pl.ds(off, tk), :] = jnp.where(key_scr[pl.ds(off, tk), :] >= jnp.maximum(thr, key_neg + 1),
                                                0.0, NEG_INF)
        return c

    tied = (count_ge(thr) > topk) & (thr > key_neg)
    any_tied = jnp.max(jnp.where(tied, 1.0, 0.0)) > 0.0

    @pl.when(any_tied)
    def _():
        lax.fori_loop(0, nt, mask_tile, jnp.zeros((1, tq), F32))

    @pl.when(jnp.logical_not(any_tied))
    def _():
        lax.fori_loop(0, nt, mask_tile_no_ties, 0)

    jq = q0 // LANES

    def group_body(g, c):
        pair_rows = [pl.multiple_of((g * (hg // 2) + pp) * LANES, LANES) for pp in range(hg // 2)]
        q_pairs = [head_cols(qt_ref[0, pl.ds(pr, LANES), :] * (hd ** -0.5)) for pr in pair_rows]

        def logits_tile(with_bias, t, ms):
            off = pl.multiple_of(t * tk, tk)
            mk = mask_scr[pl.ds(off, tk), :]
            new_ms = []
            for pp, pr in enumerate(pair_rows):
                s2 = _dot(k_ref[0, pl.ds(off, tk), pl.ds(pr, LANES)], q_pairs[pp])
                for e in range(2):
                    hh = 2 * pp + e
                    s = s2[:, e * tq:(e + 1) * tq] + mk
                    if with_bias:
                        subs = [jnp.clip((tk // LANES) * t + i - (jq - 2) + 1, 0, 3) for i in range(tk // LANES)]
                        s = s + jnp.concatenate([near_ref[sub, g * hg + hh] for sub in subs], axis=0)
                    s_scr[hh, pl.ds(off, tk), :] = s
                    m64 = _fold_rows(s, jnp.max)
                    m8 = jnp.max(m64.reshape(FOLD_ROWS // SUBLANES, SUBLANES, tq), axis=0)
                    new_ms.append(jnp.maximum(ms[hh], m8))
            return tuple(new_ms)

        t_bias = jnp.maximum(jq - 2, 0) // (tk // LANES)
        ms = tuple(jnp.full((SUBLANES, tq), NEG_INF, F32) for _ in range(hg))
        ms = lax.fori_loop(0, t_bias, functools.partial(logits_tile, False), ms)
        ms = lax.fori_loop(t_bias, nt, functools.partial(logits_tile, True), ms)
        ms = [jnp.max(m, axis=0, keepdims=True) for m in ms]

        def pv_tile(t, accs):
            off = pl.multiple_of(t * tk, tk)
            new_accs = []
            for hh in range(hg):
                pexp = jnp.exp((s_scr[hh, pl.ds(off, tk), :] - ms[hh]).astype(BF16))
                row0 = pl.multiple_of((g * hg + hh) * VT_HEAD_ROWS, 16)
                new_accs.append(accs[hh] + _dot(vt_ref[0, pl.ds(row0, VT_HEAD_ROWS), pl.ds(off, tk)], pexp))
            return tuple(new_accs)

        accs = lax.fori_loop(0, nt, pv_tile, tuple(jnp.zeros((VT_HEAD_ROWS, tq), F32) for _ in range(hg)))
        for hh in range(hg):
            row0 = pl.multiple_of((g * hg + hh) * hd, hd)
            ot_scr[pl.ds(row0, hd), :] = accs[hh][:hd] / accs[hh][hd:hd + 1]
        return c

    lax.fori_loop(0, C_HEADS // hg, group_body, 0)
    mix = _dot(ot_scr[...].T.astype(BF16), wout_ref[...])
    xo_ref[0] = x_ref[0] + gate_ref[0] * mix


def _order_key_of(value):
    bits = int(np.float32(value).view(np.int32))
    return bits ^ ((bits >> 31) & 0x7FFFFFFF)


def _rel_bucket(rel):
    half = REL_BUCKETS // 2
    max_exact = half // 2
    n = jnp.abs(rel)
    nf = jnp.maximum(n, 1).astype(F32)
    large = max_exact + (jnp.log(nf / max_exact) / math.log(REL_MAX_DIST / max_exact)
                         * (half - max_exact)).astype(jnp.int32)
    large = jnp.minimum(large, half - 1)
    return jnp.where(rel > 0, half, 0) + jnp.where(n < max_exact, n, large)


def _attn(x, gate, qt, qit, wit, k_all, vt_all, ki_all, rel_bias, w_out, past, l_len):
    b, t, d = x.shape
    lp = k_all.shape[1]
    tq = LANES
    topk = min(TOPK_MAX, l_len // 4)
    assert t % tq == 0 and past % LANES == 0 and lp % KEY_TILE == 0 and lp >= l_len
    assert 3 * LANES - (LANES - 1) >= REL_MAX_DIST
    rel = ((jnp.arange(3) - 2) * LANES)[:, None, None] + jnp.arange(LANES)[None, :, None] - jnp.arange(tq)[None, None, :]
    far = rel_bias[_rel_bucket(jnp.int32(-REL_MAX_DIST))]
    near = jnp.transpose(rel_bias[_rel_bucket(rel)] - far, (0, 3, 1, 2)).astype(F32)
    near = jnp.concatenate([jnp.zeros_like(near[:1]), near], axis=0)
    tri = jnp.asarray(np.tril(np.ones((KEY_TILE, KEY_TILE), np.float32)), BF16)
    tok = lambda n: pl.BlockSpec((1, tq, n), lambda i, j: (i, j, 0))
    tok_t = lambda n: pl.BlockSpec((1, n, tq), lambda i, j: (i, 0, j))
    resident = lambda shape: pl.BlockSpec((1,) + shape, lambda i, j: (i, 0, 0), pipeline_mode=pl.Buffered(1))
    return pl.pallas_call(
        functools.partial(_attn_kernel, past=past, l_len=l_len, topk=topk),
        grid=(b, t // tq),
        in_specs=[tok(d), pl.BlockSpec((1, 1, d), lambda i, j: (i, 0, 0)), tok_t(C_WIDTH), tok_t(IDX_HEADS * IDX_DIM),
                  tok_t(IDX_HEADS), resident((lp, C_WIDTH)), resident((C_HEADS * VT_HEAD_ROWS, lp)),
                  resident((lp, LANES)),
                  _const_spec((4, C_HEADS, LANES, tq)), _const_spec((KEY_TILE, KEY_TILE)),
                  _const_spec((C_WIDTH, d))],
        out_specs=tok(d),
        out_shape=jax.ShapeDtypeStruct((b, t, d), F32),
        scratch_shapes=[pltpu.VMEM((lp, tq), jnp.int32), pltpu.VMEM((lp, tq), F32),
                        pltpu.VMEM((ATTN_HEAD_GROUP, lp, tq), F32), pltpu.VMEM((C_WIDTH, tq), F32)],
        compiler_params=_cparams(("arbitrary", "arbitrary")),
        name="dsa_attn",
    )(x, gate, qt, qit, wit, k_all, vt_all, ki_all, near, tri, w_out)


def _prep_params(p):
    d = D_MODEL
    aw = A_WIDTH
    out = {'even': [], 'odd': [], 'ffn': []}
    ind64 = jnp.asarray(np.kron(np.eye(aw // A_HEAD_DIM), np.ones((A_HEAD_DIM, A_HEAD_DIM))), BF16)
    ind128 = jnp.asarray(np.kron(np.eye(B_WIDTH // B_VAL_DIM), np.ones((B_VAL_DIM, B_VAL_DIM))), BF16)
    for j in range(p['w_in_ab'].shape[0]):
        w = p['w_in_ab'][j]
        o = A_COLS
        lora = jnp.pad(w[:, 3 * aw:A_COLS], ((0, 0), (0, LORA_PAD - LORA_W)))
        xal = jnp.pad(w[:, o + 2 * B_KEY_WIDTH + B_WIDTH:o + 2 * B_KEY_WIDTH + B_WIDTH + GLA_RANK],
                      ((0, 0), (0, XAL_PAD - GLA_RANK)))
        w_in = jnp.concatenate([w[:, :3 * aw], w[:, o:o + 2 * B_KEY_WIDTH + B_WIDTH],
                                w[:, o + 2 * B_KEY_WIDTH + B_WIDTH + GLA_RANK:], lora, xal], axis=1).astype(BF16)
        mu = p['mu_a'][j]

        def lora_rows(m, r0):
            return jnp.pad(m, ((r0, LORA_PAD - r0 - m.shape[0]), (0, 0)))

        out['even'].append({
            'w_in': w_in,
            'mu_rkv': mu[None, :3 * aw],
            'mu_l': jnp.pad(mu[None, 3 * aw:], ((0, 0), (0, LORA_PAD - LORA_W))),
            'w0': p['w0'][j][None], 'a0': p['a0'][j][None],
            'wd': lora_rows(p['w_decay_up'][j], 0),
            'wi': lora_rows(p['w_iclr_up'][j], DECAY_LORA),
            'wg': lora_rows(p['w_gate_up'][j], DECAY_LORA + ICLR_LORA),
            'k_k': p['k_k'][j][None], 'k_a': p['k_a'][j][None], 'r_k': p['r_k'][j].reshape(1, aw),
            'gn_a_g': p['gn_a_g'][j][None], 'gn_a_b': p['gn_a_b'][j][None], 'gn_b_g': p['gn_b_g'][j][None],
            'alpha_up': jnp.pad(p['alpha_up'][j], ((0, XAL_PAD - GLA_RANK), (0, 0))),
            'alpha_b': p['alpha_b'][j][None],
            'w_out': p['w_out_ab'][j].astype(BF16),
            'ind64': ind64, 'ind128': ind128,
        })
    for j in range(p['w_in_c'].shape[0]):
        w = p['w_in_c'][j]
        o4 = 3 * C_WIDTH + IDX_HEADS * IDX_DIM
        ki = w[:, o4:o4 + IDX_DIM]
        wi = jnp.pad(w[:, o4 + IDX_DIM:], ((0, 0), (0, LANES - IDX_HEADS)))
        out['odd'].append({
            'w_in': jnp.concatenate([w[:, :o4], ki, ki, wi], axis=1).astype(BF16),
            'w_out': p['w_out_c'][j].astype(BF16),
        })
    for i in range(DEPTH):
        out['ffn'].append({
            'w_up': p['w_ffn_up'][i].astype(BF16), 'conv_w': p['ffn_conv_w'][i], 'conv_b': p['ffn_conv_b'][i][None],
            'w_down': p['w_ffn_down'][i].astype(BF16),
        })
    return out


def _token_tile(t):
    return min(512, t)


def _even_layer(x, mods, gain, shift_buf, wkv0, gla0, prm):
    b, t, d = x.shape
    aw = A_WIDTH
    tm = _token_tile(t)
    sh_m, sc_m, gt_m = mods
    kw2 = 2 * B_KEY_WIDTH
    plain = ((F32, False),)
    cols = [(0, 3 * aw, plain), (3 * aw, kw2, plain), (3 * aw + kw2, B_WIDTH, plain),
            (3 * aw + kw2 + B_WIDTH, B_WIDTH, plain), (3 * aw + kw2 + 2 * B_WIDTH, LORA_PAD + XAL_PAD, plain)]
    rkv, qk_b, v_b, r_b, lx = _proj(x, gain, sh_m, sc_m, prm['w_in'], cols, tm)
    new_shift = jnp.concatenate([rkv[:, t - 1:, :], lx[:, t - 1:, :LORA_W]], axis=-1)
    sb_rkv = shift_buf[:, :, :3 * aw]
    sb_l = jnp.pad(shift_buf[:, :, 3 * aw:], ((0, 0), (0, 0), (0, LORA_PAD - LORA_W)))
    x6, g, bonus = _pre_even(rkv, lx, sb_rkv, sb_l, prm, tm)
    chains = b * A_HEADS
    assert chains == LANES
    x6t = jnp.swapaxes(x6.reshape(6, t, chains, A_HEAD_DIM), 2, 3)
    s0 = jnp.transpose(wkv0, (2, 3, 0, 1)).reshape(A_HEAD_DIM, A_HEAD_DIM, chains)
    ya_t, s_fin = _rwkv(x6t, s0, min(RWKV_TT, t))
    ya = jnp.transpose(ya_t.reshape(t, A_HEAD_DIM, b, A_HEADS), (2, 0, 3, 1)).reshape(b, t, aw)
    wkv_new = jnp.transpose(s_fin.reshape(A_HEAD_DIM, A_HEAD_DIM, b, A_HEADS), (2, 3, 0, 1))
    ob, gla_t = _gla(qk_b, v_b, lx, prm['alpha_up'], prm['alpha_b'], jnp.swapaxes(gla0, 2, 3))
    gla_new = jnp.swapaxes(gla_t, 2, 3)
    x = _post_even(x, gt_m, ya, bonus, g, ob, r_b, prm, tm)
    return x, new_shift, wkv_new, gla_new


def _odd_layer(x, mods, gain, k_cache, v_cache, ki_cache, rel_bias, prm):
    b, t, d = x.shape
    tp = -(-t // LANES) * LANES
    xp = jnp.pad(x, ((0, 0), (0, tp - t), (0, 0)))
    tm = _token_tile(tp)
    sh_m, sc_m, gt_m = mods
    cw = C_WIDTH
    iw = IDX_HEADS * IDX_DIM
    cols = [(0, cw, ((BF16, True),)),
            (cw, cw, ((F32, False), (BF16, False))),
            (2 * cw, cw, ((F32, False), (BF16, True))),
            (3 * cw, iw, ((BF16, True),)),
            (3 * cw + iw, LANES, ((F32, False), (BF16, False))),
            (3 * cw + iw + LANES, LANES, ((F32, True),))]
    qt, k, k_bf, v, vt_bf, qit, kiki, kiki_bf, wit = _proj(xp, gain, sh_m, sc_m, prm['w_in'], cols, tm)
    past = k_cache.shape[1]
    l_len = past + t
    lp = -(-l_len // KEY_TILE) * KEY_TILE
    kc = k_cache.reshape(b, past, cw).astype(BF16)
    vct = jnp.swapaxes(v_cache.reshape(b, past, cw).astype(BF16), 1, 2)
    kic = jnp.concatenate([ki_cache, ki_cache], axis=-1).astype(BF16)
    pad = ((0, 0), (0, lp - l_len), (0, 0))
    k_all = jnp.pad(jnp.concatenate([kc, k_bf[:, :t]], axis=1), pad)
    vt_heads = jnp.concatenate([vct, vt_bf[:, :, :t]], axis=2).reshape(b, C_HEADS, C_HEAD_DIM, l_len)
    extra = jnp.zeros((b, C_HEADS, VT_HEAD_ROWS - C_HEAD_DIM, l_len), BF16).at[:, :, 0].set(1.0)
    vt_all = jnp.concatenate([vt_heads, extra], axis=2).reshape(b, C_HEADS * VT_HEAD_ROWS, l_len)
    vt_all = jnp.pad(vt_all, ((0, 0), (0, 0), (0, lp - l_len)))
    ki_all = jnp.pad(jnp.concatenate([kic, kiki_bf[:, :t]], axis=1), pad)
    xo = _attn(xp, gt_m, qt, qit, wit, k_all, vt_all, ki_all, rel_bias, prm['w_out'], past, l_len)
    return (xo[:, :t], k[:, :t].reshape(b, t, C_HEADS, C_HEAD_DIM), v[:, :t].reshape(b, t, C_HEADS, C_HEAD_DIM),
            kiki[:, :t, :IDX_DIM])


def _run_group(x, mod, shift_bufs, wkv_states, gla_states, k_caches, v_caches, ki_caches, ffn_bufs, raw, prm):
    b, t, d = x.shape
    tm = _token_tile(t)
    shifts, wkvs, glas, ks, vs, kis, ffns = [], [], [], [], [], [], []
    for i in range(DEPTH):
        m6 = [mod[i][:, None, n * d:(n + 1) * d] for n in range(N_MOD)]
        j = i // 2
        gain_mix = raw['norm_mix'][i][None]
        if i % 2 == 0:
            x, s_new, wkv_new, gla_new = _even_layer(x, m6[:3], gain_mix, shift_bufs[j], wkv_states[j],
                                                     gla_states[j], prm['even'][j])
            shifts.append(s_new)
            wkvs.append(wkv_new)
            glas.append(gla_new)
        else:
            x, k_new, v_new, ki_new = _odd_layer(x, m6[:3], gain_mix, k_caches[j], v_caches[j], ki_caches[j],
                                                 raw['rel_bias'], prm['odd'][j])
            ks.append(k_new)
            vs.append(v_new)
            kis.append(ki_new)
        f = prm['ffn'][i]
        x, f_buf = _ffn(x, raw['norm_ffn'][i][None], m6[3], m6[4], m6[5], ffn_bufs[i], f['w_up'], f['conv_w'],
                        f['conv_b'], f['w_down'], raw['norm_final'][None], min(FFN_TOKEN_TILE, t),
                        final=(i == DEPTH - 1))
        ffns.append(f_buf)
    return (x, jnp.stack(shifts), jnp.stack(wkvs), jnp.stack(glas), jnp.stack(ks), jnp.stack(vs),
            jnp.stack(kis), jnp.stack(ffns))


def kernel(x_prompt, x_sample, c_prompt, c_sample, state_shift_ab, state_wkv, state_gla, cache_k, cache_v, cache_kidx, state_ffn_conv, w_ada, b_ada, norm_mix, norm_ffn, norm_final, w_in_ab, mu_a, w0, w_decay_up, a0, w_iclr_up, w_gate_up, k_k, k_a, r_k, gn_a_g, gn_a_b, alpha_up, alpha_b, gn_b_g, w_out_ab, w_in_c, rel_bias, w_out_c, w_ffn_up, ffn_conv_w, ffn_conv_b, w_ffn_down):
    raw = {
        'norm_mix': norm_mix, 'norm_ffn': norm_ffn, 'norm_final': norm_final, 'rel_bias': rel_bias,
        'w_in_ab': w_in_ab, 'mu_a': mu_a, 'w0': w0, 'w_decay_up': w_decay_up, 'a0': a0, 'w_iclr_up': w_iclr_up,
        'w_gate_up': w_gate_up, 'k_k': k_k, 'k_a': k_a, 'r_k': r_k, 'gn_a_g': gn_a_g, 'gn_a_b': gn_a_b,
        'alpha_up': alpha_up, 'alpha_b': alpha_b, 'gn_b_g': gn_b_g, 'w_out_ab': w_out_ab, 'w_in_c': w_in_c,
        'w_out_c': w_out_c, 'w_ffn_up': w_ffn_up, 'ffn_conv_w': ffn_conv_w, 'ffn_conv_b': ffn_conv_b,
        'w_ffn_down': w_ffn_down,
    }
    prm = _prep_params(raw)
    n_even = w_in_ab.shape[0]
    n_odd = w_in_c.shape[0]
    dt = x_prompt.dtype
    bp = x_prompt.shape[0]
    mod = _ada(jnp.concatenate([c_prompt, c_sample], axis=0), w_ada, b_ada)
    mod_p, mod_s = mod[:, :bp], mod[:, bp:]
    out_p = _run_group(
        x_prompt, mod_p,
        jnp.zeros((n_even, bp, 1, A_COLS), dt),
        jnp.zeros((n_even, bp, A_HEADS, A_HEAD_DIM, A_HEAD_DIM), dt),
        jnp.zeros((n_even, bp, B_HEADS, B_KEY_DIM, B_VAL_DIM), dt),
        jnp.zeros((n_odd, bp, 0, C_HEADS, C_HEAD_DIM), dt),
        jnp.zeros((n_odd, bp, 0, C_HEADS, C_HEAD_DIM), dt),
        jnp.zeros((n_odd, bp, 0, IDX_DIM), dt),
        jnp.zeros((DEPTH, bp, CONV_W - 1, 2 * D_FF), dt),
        raw, prm)
    out_s = _run_group(x_sample, mod_s, state_shift_ab, state_wkv, state_gla, cache_k, cache_v, cache_kidx,
                       state_ffn_conv, raw, prm)
    res = []
    for a, b in zip(out_p, out_s):
        res.extend([a, b])
    return tuple(res)
```

```python
import functools
import math

import jax
import jax.numpy as jnp
import numpy as np
from jax import lax
from jax.experimental import pallas as pl
from jax.experimental.pallas import tpu as pltpu

F32 = jnp.float32
BF16 = jnp.bfloat16
HI = lax.Precision.HIGHEST

D_MODEL = 1024
DEPTH = 4
CHUNK = 64
A_HEADS = 8
A_HEAD_DIM = 64
A_WIDTH = A_HEADS * A_HEAD_DIM
DECAY_LORA = 32
ICLR_LORA = 32
GATE_LORA = 96
LORA_W = DECAY_LORA + ICLR_LORA + GATE_LORA
A_COLS = 3 * A_WIDTH + LORA_W
B_HEADS = 4
B_KEY_DIM = 64
B_VAL_DIM = 128
B_KEY_WIDTH = B_HEADS * B_KEY_DIM
B_WIDTH = B_HEADS * B_VAL_DIM
GLA_RANK = 16
GLA_TAU = 16.0
C_HEADS = 16
C_HEAD_DIM = 64
C_WIDTH = C_HEADS * C_HEAD_DIM
IDX_HEADS = 8
IDX_DIM = 64
TOPK_MAX = 256
QUERY_BLOCK = 128
REL_BUCKETS = 32
REL_MAX_DIST = 256
D_FF = 2816
CONV_W = 3
N_MOD = 6
RMS_EPS = 1e-6
GN_EPS = 64e-5
NEG_INF = -1e30

LANES = 128
SUBLANES = 8
VMEM_LIMIT = 56 * 1024 * 1024

LORA_PAD = 256
XAL_PAD = 128
KEY_TILE = 512
ATTN_HEAD_GROUP = 8
FOLD_ROWS = 64
VT_HEAD_ROWS = C_HEAD_DIM + 16
FFN_CHUNK = 256
FFN_TOKEN_TILE = 1024
RWKV_TT = 32
GLA_CHUNKS_PER_STEP = 4


def _cparams(sem):
    return pltpu.CompilerParams(dimension_semantics=sem, vmem_limit_bytes=VMEM_LIMIT)


def _const_spec(shape):
    nd = len(shape)
    return pl.BlockSpec(shape, lambda *_: (0,) * nd, pipeline_mode=pl.Buffered(1))


def _sigmoid(x):
    return 1.0 / (1.0 + jnp.exp(-x))


def _silu(x):
    return x * _sigmoid(x)


def _softplus(x):
    return jnp.maximum(x, 0.0) + jnp.log(1.0 + jnp.exp(-jnp.abs(x)))


def _dot(a, b, precision=None):
    return jnp.dot(a, b, preferred_element_type=F32, precision=precision)


def _dot_nt(a, b):
    return lax.dot_general(a, b, (((1,), (1,)), ((), ())), preferred_element_type=F32)


def _seg_sum(x, ind_bf16):
    hi = x.astype(BF16)
    lo = (x - hi.astype(F32)).astype(BF16)
    return _dot(hi, ind_bf16) + _dot(lo, ind_bf16)


def _modulated_norm(x, gain, shift, scale):
    var = jnp.mean(x * x, axis=-1, keepdims=True)
    h = x * lax.rsqrt(var + RMS_EPS) * gain
    return h * (1.0 + scale) + shift


def _ada_kernel(c_ref, w_ref, b_ref, o_ref):
    c = c_ref[...]
    o_ref[0] = _dot(_silu(c), w_ref[0], HI) + b_ref[0]


def _ada(c, w_ada, b_ada):
    m, d = c.shape
    depth, _, n = w_ada.shape
    tn = 1536
    return pl.pallas_call(
        _ada_kernel,
        grid=(depth, n // tn),
        in_specs=[pl.BlockSpec((m, d), lambda l, j: (0, 0)),
                  pl.BlockSpec((1, d, tn), lambda l, j: (l, 0, j)),
                  pl.BlockSpec((1, 1, tn), lambda l, j: (l, 0, j))],
        out_specs=pl.BlockSpec((1, m, tn), lambda l, j: (l, 0, j)),
        out_shape=jax.ShapeDtypeStruct((depth, m, n), F32),
        compiler_params=_cparams(("arbitrary", "arbitrary")),
        name="ada_mod",
    )(c, w_ada, b_ada.reshape(depth, 1, n))


def _proj_kernel(x_ref, gain_ref, shift_ref, scale_ref, w_ref, *out_refs, groups):
    h = _modulated_norm(x_ref[0], gain_ref[...], shift_ref[0], scale_ref[0]).astype(BF16)
    refs = iter(out_refs)
    for c0, n, forms in groups:
        y = _dot(h, w_ref[:, c0:c0 + n])
        for dt, transposed in forms:
            next(refs)[0] = (y.T if transposed else y).astype(dt)


def _proj(x, gain, shift, scale, w, groups, tm):
    b, t, d = x.shape
    n_all = w.shape[1]
    mod_spec = pl.BlockSpec((1, 1, d), lambda i, j: (i, 0, 0))
    outs = [(n, dt, tr) for _, n, forms in groups for dt, tr in forms]
    return pl.pallas_call(
        functools.partial(_proj_kernel, groups=groups),
        grid=(b, t // tm),
        in_specs=[pl.BlockSpec((1, tm, d), lambda i, j: (i, j, 0)),
                  _const_spec((1, d)), mod_spec, mod_spec,
                  _const_spec((d, n_all))],
        out_specs=[pl.BlockSpec((1, n, tm), lambda i, j: (i, 0, j)) if tr else
                   pl.BlockSpec((1, tm, n), lambda i, j: (i, j, 0)) for n, _, tr in outs],
        out_shape=[jax.ShapeDtypeStruct((b, n, t) if tr else (b, t, n), dt) for n, dt, tr in outs],
        compiler_params=_cparams(("arbitrary", "arbitrary")),
        name="norm_proj",
    )(x, gain, shift, scale, w)


def _ffn_kernel(x_ref, gain_ref, shift_ref, scale_ref, gate_ref, buf_ref, wup_ref, cw_ref, cb_ref, wdn_ref,
                fgain_ref, xo_ref, bufo_ref, carry_ref, act_ref, *, tm, final):
    t = pl.program_id(1)

    @pl.when(t == 0)
    def _():
        carry_ref[...] = buf_ref[0]

    x = x_ref[0]
    h = _modulated_norm(x, gain_ref[...], shift_ref[0], scale_ref[0]).astype(BF16)
    row = lax.broadcasted_iota(jnp.int32, (tm, FFN_CHUNK), 0)

    def conv(u, c0):
        c_old = carry_ref[0:1, c0:c0 + FFN_CHUNK]
        c_new = carry_ref[1:2, c0:c0 + FFN_CHUNK]
        prev1 = jnp.where(row == 0, c_new, pltpu.roll(u, 1, 0))
        prev2 = jnp.where(row == 0, c_old, jnp.where(row == 1, c_new, pltpu.roll(u, 2, 0)))
        carry_ref[:, c0:c0 + FFN_CHUNK] = u[tm - 2:tm, :]
        return (cb_ref[:, c0:c0 + FFN_CHUNK] + cw_ref[0:1, c0:c0 + FFN_CHUNK] * prev2
                + cw_ref[1:2, c0:c0 + FFN_CHUNK] * prev1 + cw_ref[2:3, c0:c0 + FFN_CHUNK] * u)

    for c in range(D_FF // FFN_CHUNK):
        g0 = c * FFN_CHUNK
        v0 = D_FF + g0
        gate = conv(_dot(h, wup_ref[:, g0:g0 + FFN_CHUNK]), g0)
        val = conv(_dot(h, wup_ref[:, v0:v0 + FFN_CHUNK]), v0)
        act_ref[:, g0:g0 + FFN_CHUNK] = (_silu(gate) * val).astype(BF16)

    y = x + gate_ref[0] * _dot(act_ref[...], wdn_ref[...])
    if final:
        y = y * lax.rsqrt(jnp.mean(y * y, axis=-1, keepdims=True) + RMS_EPS) * fgain_ref[...]
    xo_ref[0] = y

    @pl.when(t == pl.num_programs(1) - 1)
    def _():
        bufo_ref[0] = carry_ref[...]


def _ffn(x, gain, shift, scale, gate, buf, w_up, conv_w, conv_b, w_down, final_gain, tm, final):
    b, t, d = x.shape
    f2 = 2 * D_FF
    mod_spec = pl.BlockSpec((1, 1, d), lambda i, j: (i, 0, 0))
    buf_spec = pl.BlockSpec((1, CONV_W - 1, f2), lambda i, j: (i, 0, 0))
    return pl.pallas_call(
        functools.partial(_ffn_kernel, tm=tm, final=final),
        grid=(b, t // tm),
        in_specs=[pl.BlockSpec((1, tm, d), lambda i, j: (i, j, 0)),
                  _const_spec((1, d)), mod_spec, mod_spec, mod_spec, buf_spec,
                  _const_spec((d, f2)), _const_spec((CONV_W, f2)), _const_spec((1, f2)),
                  _const_spec((D_FF, d)), _const_spec((1, d))],
        out_specs=[pl.BlockSpec((1, tm, d), lambda i, j: (i, j, 0)), buf_spec],
        out_shape=[jax.ShapeDtypeStruct((b, t, d), F32),
                   jax.ShapeDtypeStruct((b, CONV_W - 1, f2), F32)],
        scratch_shapes=[pltpu.VMEM((CONV_W - 1, f2), F32), pltpu.VMEM((tm, D_FF), BF16)],
        compiler_params=_cparams(("arbitrary", "arbitrary")),
        name="conv_ffn",
    )(x, gain, shift, scale, gate, buf, w_up, conv_w, conv_b, w_down, final_gain)


def _pre_even_kernel(rkv_ref, lx_ref, sb_rkv_ref, sb_l_ref, mu_rkv_ref, mu_l_ref, w0_ref, wd_ref, a0_ref, wi_ref,
                     wg_ref, kk_ref, ka_ref, rk_ref, ind_ref, x6_ref, g_ref, bonus_ref, c_rkv, c_l, *, tm):
    t = pl.program_id(1)

    @pl.when(t == 0)
    def _():
        c_rkv[...] = sb_rkv_ref[0]
        c_l[...] = sb_l_ref[0]

    def shifted(p, carry_ref, mu):
        row = lax.broadcasted_iota(jnp.int32, p.shape, 0)
        prev = jnp.where(row == 0, carry_ref[...], pltpu.roll(p, 1, 0))
        carry_ref[...] = p[tm - 1:tm, :]
        return p + (prev - p) * mu

    pa = shifted(rkv_ref[0], c_rkv, mu_rkv_ref[...])
    lora = shifted(lx_ref[0][:, :LORA_PAD], c_l, mu_l_ref[...])
    r = pa[:, :A_WIDTH]
    k = pa[:, A_WIDTH:2 * A_WIDTH]
    v = pa[:, 2 * A_WIDTH:]
    w_log = -_softplus(-(w0_ref[...] + _dot(jnp.tanh(lora), wd_ref[...], HI))) - 0.5
    decay = jnp.exp(-jnp.exp(w_log))
    a = _sigmoid(a0_ref[...] + _dot(lora, wi_ref[...], HI))
    g = _dot(_sigmoid(lora), wg_ref[...], HI)
    kk = k * kk_ref[...]
    k2 = k * (1.0 + (a - 1.0) * ka_ref[...])
    ind = ind_ref[...]
    kk = kk * lax.rsqrt(jnp.maximum(_seg_sum(kk * kk, ind), 1e-12))
    bonus = _seg_sum(r * k2 * rk_ref[...], ind) * v
    x6_ref[0] = r
    x6_ref[1] = decay
    x6_ref[2] = k2
    x6_ref[3] = v
    x6_ref[4] = kk
    x6_ref[5] = kk * a
    g_ref[0] = g
    bonus_ref[0] = bonus


def _pre_even(rkv, lx, sb_rkv, sb_l, prm, tm):
    b, t, _ = rkv.shape
    aw = A_WIDTH
    row = lambda n: _const_spec((1, n))
    return pl.pallas_call(
        functools.partial(_pre_even_kernel, tm=tm),
        grid=(b, t // tm),
        in_specs=[pl.BlockSpec((1, tm, 3 * aw), lambda i, j: (i, j, 0)),
                  pl.BlockSpec((1, tm, LORA_PAD + XAL_PAD), lambda i, j: (i, j, 0)),
                  pl.BlockSpec((1, 1, 3 * aw), lambda i, j: (i, 0, 0)),
                  pl.BlockSpec((1, 1, LORA_PAD), lambda i, j: (i, 0, 0)),
                  row(3 * aw), row(LORA_PAD), row(aw), _const_spec((LORA_PAD, aw)), row(aw),
                  _const_spec((LORA_PAD, aw)), _const_spec((LORA_PAD, aw)), row(aw), row(aw), row(aw),
                  _const_spec((aw, aw))],
        out_specs=[pl.BlockSpec((6, tm, aw), lambda i, j: (0, j, i)),
                   pl.BlockSpec((1, tm, aw), lambda i, j: (i, j, 0)),
                   pl.BlockSpec((1, tm, aw), lambda i, j: (i, j, 0))],
        out_shape=[jax.ShapeDtypeStruct((6, t, b * aw), F32),
                   jax.ShapeDtypeStruct((b, t, aw), F32),
                   jax.ShapeDtypeStruct((b, t, aw), F32)],
        scratch_shapes=[pltpu.VMEM((1, 3 * aw), F32), pltpu.VMEM((1, LORA_PAD), F32)],
        compiler_params=_cparams(("arbitrary", "arbitrary")),
        name="rwkv_prep",
    )(rkv, lx, sb_rkv, sb_l, prm['mu_rkv'], prm['mu_l'], prm['w0'], prm['wd'], prm['a0'], prm['wi'], prm['wg'],
      prm['k_k'], prm['k_a'], prm['r_k'], prm['ind64'])


def _rwkv_kernel(x_ref, s0_ref, y_ref, so_ref, s_scr, *, tt):
    n = A_HEAD_DIM
    g = n // SUBLANES
    i = pl.program_id(0)

    @pl.when(i == 0)
    def _():
        s_scr[...] = s0_ref[...]

    def allsum(x):
        x = x + pltpu.roll(x, 4, 0)
        x = x + pltpu.roll(x, 2, 0)
        return x + pltpu.roll(x, 1, 0)

    def ksum(x3):
        return allsum(jnp.sum(x3, axis=0))

    def step(t, carry):
        def ld(j):
            return x_ref[j, t].reshape(g, SUBLANES, LANES)
        r, w, k, kk, b = ld(0), ld(1), ld(2), ld(4), ld(5)
        wr = w * r
        br = ksum(b * r)
        kr = ksum(k * r)

        def vbody(v, c):
            sv = s_scr[v].reshape(g, SUBLANES, LANES)
            sa = ksum(sv * kk)
            z = ksum(sv * wr)
            vv = jnp.broadcast_to(x_ref[3, t, pl.ds(v, 1), :], (SUBLANES, LANES))
            s_scr[v] = (sv * w - sa[None] * b + vv[None] * k).reshape(n, LANES)
            y = z - sa * br + vv * kr
            y_ref[t, pl.ds(v, 1), :] = y[0:1]
            return c

        lax.fori_loop(0, n, vbody, 0, unroll=8)
        return carry

    lax.fori_loop(0, tt, step, 0)

    @pl.when(i == pl.num_programs(0) - 1)
    def _():
        so_ref[...] = s_scr[...]


def _rwkv(x6t, s0, tt):
    t = x6t.shape[1]
    n = A_HEAD_DIM
    return pl.pallas_call(
        functools.partial(_rwkv_kernel, tt=tt),
        grid=(t // tt,),
        in_specs=[pl.BlockSpec((6, tt, n, LANES), lambda i: (0, i, 0, 0)),
                  pl.BlockSpec((n, n, LANES), lambda i: (0, 0, 0))],
        out_specs=[pl.BlockSpec((tt, n, LANES), lambda i: (i, 0, 0)),
                   pl.BlockSpec((n, n, LANES), lambda i: (0, 0, 0))],
        out_shape=[jax.ShapeDtypeStruct((t, n, LANES), F32),
                   jax.ShapeDtypeStruct((n, n, LANES), F32)],
        scratch_shapes=[pltpu.VMEM((n, n, LANES), F32)],
        compiler_params=_cparams(("arbitrary",)),
        name="rwkv_scan",
    )(x6t, s0)


def _gla_tables(c):
    idx = np.arange(c)
    t = idx[:, None]
    r = idx[None, :]
    mats = [(r <= t), (r > t)]
    masks = [np.eye(c, dtype=bool)]
    m = c // 2
    while m >= 1:
        mid = (idx // (2 * m)) * 2 * m + m - 1
        right = (idx % (2 * m)) >= m
        mats.append(right[:, None] & (r > mid[:, None]) & (r <= t))
        mats.append((~right)[:, None] & (r > t) & (r <= mid[:, None]))
        masks.append(((idx // (2 * m))[:, None] == (idx // (2 * m))[None, :]) & right[:, None] & (~right)[None, :])
        m //= 2
    return (np.concatenate(mats, 0).astype(np.float32), np.stack(masks).astype(np.float32))


def _gla_kernel(qk_ref, v_ref, lx_ref, au_ref, ab_ref, mall_ref, masks_ref, s0_ref, o_ref, so_ref, s_scr, *, c, nlev,
                nc):
    i = pl.program_id(1)

    @pl.when(i == 0)
    def _():
        s_scr[...] = s0_ref[0]

    z_all = _dot(lx_ref[0], au_ref[...], HI) + ab_ref[...]
    g_all = -_softplus(-z_all) * (1.0 / GLA_TAU)
    mall = mall_ref[...]
    for j in range(nc):
        rows = slice(j * c, (j + 1) * c)
        g = g_all[rows]
        g_hi = g.astype(BF16)
        g_mid = (g - g_hi.astype(F32)).astype(BF16)
        g_lo = (g - g_hi.astype(F32) - g_mid.astype(F32)).astype(BF16)
        d_all = _dot(mall, g_hi) + _dot(mall, g_mid) + _dot(mall, g_lo)
        qk = qk_ref[0, rows, :]
        q = qk[:, :B_KEY_WIDTH] * (B_KEY_DIM ** -0.5)
        k = qk[:, B_KEY_WIDTH:]
        v = v_ref[0, rows, :]
        b_inc = d_all[0:c]
        b_rev = d_all[c:2 * c]
        q_in = (q * jnp.exp(b_inc)).astype(BF16)
        k_st = (k * jnp.exp(b_rev)).astype(BF16)
        dec_last = jnp.exp(b_inc[c - 1:c, :])
        qd = [q.astype(BF16)]
        kd = [k.astype(BF16)]
        for lev in range(nlev):
            dq = d_all[(2 + 2 * lev) * c:(3 + 2 * lev) * c]
            dk = d_all[(3 + 2 * lev) * c:(4 + 2 * lev) * c]
            qd.append((q * jnp.exp(dq)).astype(BF16))
            kd.append((k * jnp.exp(dk)).astype(BF16))
        for h in range(B_HEADS):
            ks = slice(h * B_KEY_DIM, (h + 1) * B_KEY_DIM)
            vs = slice(h * B_VAL_DIM, (h + 1) * B_VAL_DIM)
            att = jnp.zeros((c, c), F32)
            for lev in range(nlev + 1):
                att = att + masks_ref[lev] * _dot_nt(qd[lev][:, ks], kd[lev][:, ks])
            vh = v[:, vs]
            intra = _dot(att.astype(BF16), vh.astype(BF16))
            st_add = _dot(vh.T.astype(BF16), k_st[:, ks])
            st = s_scr[h]
            o_ref[0, rows, vs] = _dot_nt(q_in[:, ks], st.astype(BF16)) + intra
            s_scr[h] = st * dec_last[:, ks] + st_add

    @pl.when(i == pl.num_programs(1) - 1)
    def _():
        so_ref[0] = s_scr[...]


def _gla(qk, v, lx, alpha_up, alpha_b, s0t):
    b, t, _ = qk.shape
    c = min(CHUNK, t)
    nlev = int(round(math.log2(c)))
    mall, masks = _gla_tables(c)
    kw, vw = B_KEY_WIDTH, B_WIDTH
    nc = math.gcd(GLA_CHUNKS_PER_STEP, t // c)
    rows = nc * c
    st_spec = pl.BlockSpec((1, B_HEADS, B_VAL_DIM, B_KEY_DIM), lambda i, j: (i, 0, 0, 0))
    return pl.pallas_call(
        functools.partial(_gla_kernel, c=c, nlev=nlev, nc=nc),
        grid=(b, t // rows),
        in_specs=[pl.BlockSpec((1, rows, 2 * kw), lambda i, j: (i, j, 0)),
                  pl.BlockSpec((1, rows, vw), lambda i, j: (i, j, 0)),
                  pl.BlockSpec((1, rows, XAL_PAD), lambda i, j: (i, j, LORA_PAD // XAL_PAD)),
                  _const_spec((XAL_PAD, kw)), _const_spec((1, kw)),
                  _const_spec(mall.shape), _const_spec(masks.shape), st_spec],
        out_specs=[pl.BlockSpec((1, rows, vw), lambda i, j: (i, j, 0)), st_spec],
        out_shape=[jax.ShapeDtypeStruct((b, t, vw), F32),
                   jax.ShapeDtypeStruct((b, B_HEADS, B_VAL_DIM, B_KEY_DIM), F32)],
        scratch_shapes=[pltpu.VMEM((B_HEADS, B_VAL_DIM, B_KEY_DIM), F32)],
        compiler_params=_cparams(("arbitrary", "arbitrary")),
        name="gla_chunk",
    )(qk, v, lx, alpha_up, alpha_b, jnp.asarray(mall, BF16), jnp.asarray(masks), s0t)


def _post_even_kernel(x_ref, gate_ref, ya_ref, bonus_ref, g_ref, ob_ref, rb_ref, gag_ref, gab_ref, gbg_ref,
                      ind64_ref, ind128_ref, wout_ref, xo_ref):
    ya = ya_ref[0]
    ind64 = ind64_ref[...]
    mu = _seg_sum(ya, ind64) * (1.0 / A_HEAD_DIM)
    dev = ya - mu
    var = _seg_sum(dev * dev, ind64) * (1.0 / A_HEAD_DIM)
    out_a = (dev * lax.rsqrt(var + GN_EPS) * gag_ref[...] + gab_ref[...] + bonus_ref[0]) * g_ref[0]
    ob = ob_ref[0]
    ms = _seg_sum(ob * ob, ind128_ref[...]) * (1.0 / B_VAL_DIM)
    out_b = ob * lax.rsqrt(ms + RMS_EPS) * gbg_ref[...] * _silu(rb_ref[0])
    mix = _dot(out_a.astype(BF16), wout_ref[:A_WIDTH, :]) + _dot(out_b.astype(BF16), wout_ref[A_WIDTH:, :])
    xo_ref[0] = x_ref[0] + gate_ref[0] * mix


def _post_even(x, gate, ya, bonus, g, ob, rb, prm, tm):
    b, t, d = x.shape
    aw = A_WIDTH
    tok = lambda n: pl.BlockSpec((1, tm, n), lambda i, j: (i, j, 0))
    return pl.pallas_call(
        _post_even_kernel,
        grid=(b, t // tm),
        in_specs=[tok(d), pl.BlockSpec((1, 1, d), lambda i, j: (i, 0, 0)), tok(aw), tok(aw), tok(aw), tok(B_WIDTH),
                  tok(B_WIDTH), _const_spec((1, aw)), _const_spec((1, aw)), _const_spec((1, B_WIDTH)),
                  _const_spec((aw, aw)), _const_spec((B_WIDTH, B_WIDTH)), _const_spec((aw + B_WIDTH, d))],
        out_specs=tok(d),
        out_shape=jax.ShapeDtypeStruct((b, t, d), F32),
        compiler_params=_cparams(("arbitrary", "arbitrary")),
        name="even_out",
    )(x, gate, ya, bonus, g, ob, rb, prm['gn_a_g'], prm['gn_a_b'], prm['gn_b_g'], prm['ind64'], prm['ind128'],
      prm['w_out'])


def _fold_rows(x, op):
    return op(x.reshape(x.shape[0] // FOLD_ROWS, FOLD_ROWS, x.shape[1]), axis=0)


def _attn_kernel(x_ref, gate_ref, qt_ref, qit_ref, wit_ref, k_ref, vt_ref, ki_ref, near_ref, tri_ref, wout_ref,
                 xo_ref, key_scr, mask_scr, s_scr, ot_scr, *, past, l_len, topk):
    tq = LANES
    tk = KEY_TILE
    hg = ATTN_HEAD_GROUP
    hd = C_HEAD_DIM
    q0 = past + pl.program_id(1) * tq
    col_pos = q0 + lax.broadcasted_iota(jnp.int32, (1, tq), 1)
    lim = jnp.minimum((col_pos // CHUNK + 1) * CHUNK, l_len)
    n_adm = jnp.minimum(((q0 + tq - 1) // CHUNK + 1) * CHUNK, l_len)
    nt = (n_adm + tk - 1) // tk
    key_neg = _order_key_of(NEG_INF)
    zero_half = jnp.zeros((hd, tq), BF16)

    def head_cols(pair):
        return jnp.concatenate([jnp.concatenate([pair[:hd], zero_half], axis=0),
                                jnp.concatenate([zero_half, pair[hd:]], axis=0)], axis=1)

    w_rows = [wit_ref[0, h:h + 1, :] * (IDX_HEADS ** -0.5 * IDX_DIM ** -0.5) for h in range(IDX_HEADS)]
    qi_pairs = [head_cols(qit_ref[0, p * LANES:(p + 1) * LANES, :]) for p in range(IDX_HEADS // 2)]

    def score_tile(t, c):
        off = pl.multiple_of(t * tk, tk)
        ki = ki_ref[0, pl.ds(off, tk), :]
        score = jnp.zeros((tk, tq), F32)
        for p in range(IDX_HEADS // 2):
            s2 = _dot(ki, qi_pairs[p])
            score = score + jnp.maximum(s2[:, :tq], 0.0) * w_rows[2 * p] + jnp.maximum(s2[:, tq:], 0.0) * w_rows[2 * p + 1]
        kpos = off + lax.broadcasted_iota(jnp.int32, (tk, tq), 0)
        score = jnp.where(score == 0.0, 0.0, score)
        score = jnp.where(kpos < lim, score, NEG_INF)
        bits = lax.bitcast_convert_type(score, jnp.int32)
        key_scr[pl.ds(off, tk), :] = bits ^ ((bits >> 31) & 0x7FFFFFFF)
        return c

    lax.fori_loop(0, nt, score_tile, 0)

    def count_ge(thr):
        def body(t, acc):
            off = pl.multiple_of(t * tk, tk)
            return acc + _fold_rows(jnp.where(key_scr[pl.ds(off, tk), :] >= thr, 1.0, 0.0), jnp.sum)
        acc = lax.fori_loop(0, nt, body, jnp.zeros((FOLD_ROWS, tq), F32))
        return jnp.sum(acc, axis=0, keepdims=True)

    int_min = jnp.full((1, tq), -2 ** 31, jnp.int32)
    thr = jnp.where(count_ge(jnp.zeros((1, tq), jnp.int32)) >= topk, 0, int_min)

    def bit_step(i, thr):
        cand = thr + (jnp.int32(1) << (30 - i))
        return jnp.where(count_ge(cand) >= topk, cand, thr)

    thr = lax.fori_loop(0, 31, bit_step, thr)
    need = topk - count_ge(thr + 1)

    def mask_tile(t, run):
        off = pl.multiple_of(t * tk, tk)
        key = key_scr[pl.ds(off, tk), :]
        eq = key == thr
        prefix = run + _dot(tri_ref[...], jnp.where(eq, 1.0, 0.0).astype(BF16))
        sel = ((key > thr) | (eq & (prefix <= need))) & (key > key_neg)
        mask_scr[pl.ds(off, tk), :] = jnp.where(sel, 0.0, NEG_INF)
        return prefix[tk - 1:tk, :]

    def mask_tile_no_ties(t, c):
        off = pl.multiple_of(t * tk, tk)
        mask_scr[pl.ds(off, tk), :] = jnp.where(key_scr[pl.ds(off, tk), :] >= jnp.maximum(thr, key_neg + 1),
                                                0.0, NEG_INF)
        return c

    tied = (count_ge(thr) > topk) & (thr > key_neg)
    any_tied = jnp.max(jnp.where(tied, 1.0, 0.0)) > 0.0

    @pl.when(any_tied)
    def _():
        lax.fori_loop(0, nt, mask_tile, jnp.zeros((1, tq), F32))

    @pl.when(jnp.logical_not(any_tied))
    def _():
        lax.fori_loop(0, nt, mask_tile_no_ties, 0)

    jq = q0 // LANES

    def group_body(g, c):
        pair_rows = [pl.multiple_of((g * (hg // 2) + pp) * LANES, LANES) for pp in range(hg // 2)]
        q_pairs = [head_cols(qt_ref[0, pl.ds(pr, LANES), :] * (hd ** -0.5)) for pr in pair_rows]

        def logits_tile(with_bias, t, ms):
            off = pl.multiple_of(t * tk, tk)
            mk = mask_scr[pl.ds(off, tk), :]
            new_ms = []
            for pp, pr in enumerate(pair_rows):
                s2 = _dot(k_ref[0, pl.ds(off, tk), pl.ds(pr, LANES)], q_pairs[pp])
                for e in range(2):
                    hh = 2 * pp + e
                    s = s2[:, e * tq:(e + 1) * tq] + mk
                    if with_bias:
                        subs = [jnp.clip((tk // LANES) * t + i - (jq - 2) + 1, 0, 3) for i in range(tk // LANES)]
                        s = s + jnp.concatenate([near_ref[sub, g * hg + hh] for sub in subs], axis=0)
                    s_scr[hh, pl.ds(off, tk), :] = s
                    m64 = _fold_rows(s, jnp.max)
                    m8 = jnp.max(m64.reshape(FOLD_ROWS // SUBLANES, SUBLANES, tq), axis=0)
                    new_ms.append(jnp.maximum(ms[hh], m8))
            return tuple(new_ms)

        t_bias = jnp.maximum(jq - 2, 0) // (tk // LANES)
        ms = tuple(jnp.full((SUBLANES, tq), NEG_INF, F32) for _ in range(hg))
        ms = lax.fori_loop(0, t_bias, functools.partial(logits_tile, False), ms)
        ms = lax.fori_loop(t_bias, nt, functools.partial(logits_tile, True), ms)
        ms = [jnp.max(m, axis=0, keepdims=True) for m in ms]

        def pv_tile(t, accs):
            off = pl.multiple_of(t * tk, tk)
            new_accs = []
            for hh in range(hg):
                pexp = jnp.exp((s_scr[hh, pl.ds(off, tk), :] - ms[hh]).astype(BF16))
                row0 = pl.multiple_of((g * hg + hh) * VT_HEAD_ROWS, 16)
                new_accs.append(accs[hh] + _dot(vt_ref[0, pl.ds(row0, VT_HEAD_ROWS), pl.ds(off, tk)], pexp))
            return tuple(new_accs)

        accs = lax.fori_loop(0, nt, pv_tile, tuple(jnp.zeros((VT_HEAD_ROWS, tq), F32) for _ in range(hg)))
        for hh in range(hg):
            row0 = pl.multiple_of((g * hg + hh) * hd, hd)
            ot_scr[pl.ds(row0, hd), :] = accs[hh][:hd] / accs[hh][hd:hd + 1]
        return c

    lax.fori_loop(0, C_HEADS // hg, group_body, 0)
    mix = _dot(ot_scr[...].T.astype(BF16), wout_ref[...])
    xo_ref[0] = x_ref[0] + gate_ref[0] * mix


def _order_key_of(value):
    bits = int(np.float32(value).view(np.int32))
    return bits ^ ((bits >> 31) & 0x7FFFFFFF)


def _rel_bucket(rel):
    half = REL_BUCKETS // 2
    max_exact = half // 2
    n = jnp.abs(rel)
    nf = jnp.maximum(n, 1).astype(F32)
    large = max_exact + (jnp.log(nf / max_exact) / math.log(REL_MAX_DIST / max_exact)
                         * (half - max_exact)).astype(jnp.int32)
    large = jnp.minimum(large, half - 1)
    return jnp.where(rel > 0, half, 0) + jnp.where(n < max_exact, n, large)


def _attn(x, gate, qt, qit, wit, k_all, vt_all, ki_all, rel_bias, w_out, past, l_len):
    b, t, d = x.shape
    lp = k_all.shape[1]
    tq = LANES
    topk = min(TOPK_MAX, l_len // 4)
    assert t % tq == 0 and past % LANES == 0 and lp % KEY_TILE == 0 and lp >= l_len
    assert 3 * LANES - (LANES - 1) >= REL_MAX_DIST
    rel = ((jnp.arange(3) - 2) * LANES)[:, None, None] + jnp.arange(LANES)[None, :, None] - jnp.arange(tq)[None, None, :]
    far = rel_bias[_rel_bucket(jnp.int32(-REL_MAX_DIST))]
    near = jnp.transpose(rel_bias[_rel_bucket(rel)] - far, (0, 3, 1, 2)).astype(F32)
    near = jnp.concatenate([jnp.zeros_like(near[:1]), near], axis=0)
    tri = jnp.asarray(np.tril(np.ones((KEY_TILE, KEY_TILE), np.float32)), BF16)
    tok = lambda n: pl.BlockSpec((1, tq, n), lambda i, j: (i, j, 0))
    tok_t = lambda n: pl.BlockSpec((1, n, tq), lambda i, j: (i, 0, j))
    resident = lambda shape: pl.BlockSpec((1,) + shape, lambda i, j: (i, 0, 0), pipeline_mode=pl.Buffered(1))
    return pl.pallas_call(
        functools.partial(_attn_kernel, past=past, l_len=l_len, topk=topk),
        grid=(b, t // tq),
        in_specs=[tok(d), pl.BlockSpec((1, 1, d), lambda i, j: (i, 0, 0)), tok_t(C_WIDTH), tok_t(IDX_HEADS * IDX_DIM),
                  tok_t(IDX_HEADS), resident((lp, C_WIDTH)), resident((C_HEADS * VT_HEAD_ROWS, lp)),
                  resident((lp, LANES)),
                  _const_spec((4, C_HEADS, LANES, tq)), _const_spec((KEY_TILE, KEY_TILE)),
                  _const_spec((C_WIDTH, d))],
        out_specs=tok(d),
        out_shape=jax.ShapeDtypeStruct((b, t, d), F32),
        scratch_shapes=[pltpu.VMEM((lp, tq), jnp.int32), pltpu.VMEM((lp, tq), F32),
                        pltpu.VMEM((ATTN_HEAD_GROUP, lp, tq), F32), pltpu.VMEM((C_WIDTH, tq), F32)],
        compiler_params=_cparams(("arbitrary", "arbitrary")),
        name="dsa_attn",
    )(x, gate, qt, qit, wit, k_all, vt_all, ki_all, near, tri, w_out)


def _prep_params(p):
    d = D_MODEL
    aw = A_WIDTH
    out = {'even': [], 'odd': [], 'ffn': []}
    ind64 = jnp.asarray(np.kron(np.eye(aw // A_HEAD_DIM), np.ones((A_HEAD_DIM, A_HEAD_DIM))), BF16)
    ind128 = jnp.asarray(np.kron(np.eye(B_WIDTH // B_VAL_DIM), np.ones((B_VAL_DIM, B_VAL_DIM))), BF16)
    for j in range(p['w_in_ab'].shape[0]):
        w = p['w_in_ab'][j]
        o = A_COLS
        lora = jnp.pad(w[:, 3 * aw:A_COLS], ((0, 0), (0, LORA_PAD - LORA_W)))
        xal = jnp.pad(w[:, o + 2 * B_KEY_WIDTH + B_WIDTH:o + 2 * B_KEY_WIDTH + B_WIDTH + GLA_RANK],
                      ((0, 0), (0, XAL_PAD - GLA_RANK)))
        w_in = jnp.concatenate([w[:, :3 * aw], w[:, o:o + 2 * B_KEY_WIDTH + B_WIDTH],
                                w[:, o + 2 * B_KEY_WIDTH + B_WIDTH + GLA_RANK:], lora, xal], axis=1).astype(BF16)
        mu = p['mu_a'][j]

        def lora_rows(m, r0):
            return jnp.pad(m, ((r0, LORA_PAD - r0 - m.shape[0]), (0, 0)))

        out['even'].append({
            'w_in': w_in,
            'mu_rkv': mu[None, :3 * aw],
            'mu_l': jnp.pad(mu[None, 3 * aw:], ((0, 0), (0, LORA_PAD - LORA_W))),
            'w0': p['w0'][j][None], 'a0': p['a0'][j][None],
            'wd': lora_rows(p['w_decay_up'][j], 0),
            'wi': lora_rows(p['w_iclr_up'][j], DECAY_LORA),
            'wg': lora_rows(p['w_gate_up'][j], DECAY_LORA + ICLR_LORA),
            'k_k': p['k_k'][j][None], 'k_a': p['k_a'][j][None], 'r_k': p['r_k'][j].reshape(1, aw),
            'gn_a_g': p['gn_a_g'][j][None], 'gn_a_b': p['gn_a_b'][j][None], 'gn_b_g': p['gn_b_g'][j][None],
            'alpha_up': jnp.pad(p['alpha_up'][j], ((0, XAL_PAD - GLA_RANK), (0, 0))),
            'alpha_b': p['alpha_b'][j][None],
            'w_out': p['w_out_ab'][j].astype(BF16),
            'ind64': ind64, 'ind128': ind128,
        })
    for j in range(p['w_in_c'].shape[0]):
        w = p['w_in_c'][j]
        o4 = 3 * C_WIDTH + IDX_HEADS * IDX_DIM
        ki = w[:, o4:o4 + IDX_DIM]
        wi = jnp.pad(w[:, o4 + IDX_DIM:], ((0, 0), (0, LANES - IDX_HEADS)))
        out['odd'].append({
            'w_in': jnp.concatenate([w[:, :o4], ki, ki, wi], axis=1).astype(BF16),
            'w_out': p['w_out_c'][j].astype(BF16),
        })
    for i in range(DEPTH):
        out['ffn'].append({
            'w_up': p['w_ffn_up'][i].astype(BF16), 'conv_w': p['ffn_conv_w'][i], 'conv_b': p['ffn_conv_b'][i][None],
            'w_down': p['w_ffn_down'][i].astype(BF16),
        })
    return out


def _token_tile(t):
    return min(512, t)


def _even_layer(x, mods, gain, shift_buf, wkv0, gla0, prm):
    b, t, d = x.shape
    aw = A_WIDTH
    tm = _token_tile(t)
    sh_m, sc_m, gt_m = mods
    kw2 = 2 * B_KEY_WIDTH
    plain = ((F32, False),)
    cols = [(0, 3 * aw, plain), (3 * aw, kw2, plain), (3 * aw + kw2, B_WIDTH, plain),
            (3 * aw + kw2 + B_WIDTH, B_WIDTH, plain), (3 * aw + kw2 + 2 * B_WIDTH, LORA_PAD + XAL_PAD, plain)]
    rkv, qk_b, v_b, r_b, lx = _proj(x, gain, sh_m, sc_m, prm['w_in'], cols, tm)
    new_shift = jnp.concatenate([rkv[:, t - 1:, :], lx[:, t - 1:, :LORA_W]], axis=-1)
    sb_rkv = shift_buf[:, :, :3 * aw]
    sb_l = jnp.pad(shift_buf[:, :, 3 * aw:], ((0, 0), (0, 0), (0, LORA_PAD - LORA_W)))
    x6, g, bonus = _pre_even(rkv, lx, sb_rkv, sb_l, prm, tm)
    chains = b * A_HEADS
    assert chains == LANES
    x6t = jnp.swapaxes(x6.reshape(6, t, chains, A_HEAD_DIM), 2, 3)
    s0 = jnp.transpose(wkv0, (2, 3, 0, 1)).reshape(A_HEAD_DIM, A_HEAD_DIM, chains)
    ya_t, s_fin = _rwkv(x6t, s0, min(RWKV_TT, t))
    ya = jnp.transpose(ya_t.reshape(t, A_HEAD_DIM, b, A_HEADS), (2, 0, 3, 1)).reshape(b, t, aw)
    wkv_new = jnp.transpose(s_fin.reshape(A_HEAD_DIM, A_HEAD_DIM, b, A_HEADS), (2, 3, 0, 1))
    ob, gla_t = _gla(qk_b, v_b, lx, prm['alpha_up'], prm['alpha_b'], jnp.swapaxes(gla0, 2, 3))
    gla_new = jnp.swapaxes(gla_t, 2, 3)
    x = _post_even(x, gt_m, ya, bonus, g, ob, r_b, prm, tm)
    return x, new_shift, wkv_new, gla_new


def _odd_layer(x, mods, gain, k_cache, v_cache, ki_cache, rel_bias, prm):
    b, t, d = x.shape
    tp = -(-t // LANES) * LANES
    xp = jnp.pad(x, ((0, 0), (0, tp - t), (0, 0)))
    tm = _token_tile(tp)
    sh_m, sc_m, gt_m = mods
    cw = C_WIDTH
    iw = IDX_HEADS * IDX_DIM
    cols = [(0, cw, ((BF16, True),)),
            (cw, cw, ((F32, False), (BF16, False))),
            (2 * cw, cw, ((F32, False), (BF16, True))),
            (3 * cw, iw, ((BF16, True),)),
            (3 * cw + iw, LANES, ((F32, False), (BF16, False))),
            (3 * cw + iw + LANES, LANES, ((F32, True),))]
    qt, k, k_bf, v, vt_bf, qit, kiki, kiki_bf, wit = _proj(xp, gain, sh_m, sc_m, prm['w_in'], cols, tm)
    past = k_cache.shape[1]
    l_len = past + t
    lp = -(-l_len // KEY_TILE) * KEY_TILE
    kc = k_cache.reshape(b, past, cw).astype(BF16)
    vct = jnp.swapaxes(v_cache.reshape(b, past, cw).astype(BF16), 1, 2)
    kic = jnp.concatenate([ki_cache, ki_cache], axis=-1).astype(BF16)
    pad = ((0, 0), (0, lp - l_len), (0, 0))
    k_all = jnp.pad(jnp.concatenate([kc, k_bf[:, :t]], axis=1), pad)
    vt_heads = jnp.concatenate([vct, vt_bf[:, :, :t]], axis=2).reshape(b, C_HEADS, C_HEAD_DIM, l_len)
    extra = jnp.zeros((b, C_HEADS, VT_HEAD_ROWS - C_HEAD_DIM, l_len), BF16).at[:, :, 0].set(1.0)
    vt_all = jnp.concatenate([vt_heads, extra], axis=2).reshape(b, C_HEADS * VT_HEAD_ROWS, l_len)
    vt_all = jnp.pad(vt_all, ((0, 0), (0, 0), (0, lp - l_len)))
    ki_all = jnp.pad(jnp.concatenate([kic, kiki_bf[:, :t]], axis=1), pad)
    xo = _attn(xp, gt_m, qt, qit, wit, k_all, vt_all, ki_all, rel_bias, prm['w_out'], past, l_len)
    return (xo[:, :t], k[:, :t].reshape(b, t, C_HEADS, C_HEAD_DIM), v[:, :t].reshape(b, t, C_HEADS, C_HEAD_DIM),
            kiki[:, :t, :IDX_DIM])


def _run_group(x, mod, shift_bufs, wkv_states, gla_states, k_caches, v_caches, ki_caches, ffn_bufs, raw, prm):
    b, t, d = x.shape
    tm = _token_tile(t)
    shifts, wkvs, glas, ks, vs, kis, ffns = [], [], [], [], [], [], []
    for i in range(DEPTH):
        m6 = [mod[i][:, None, n * d:(n + 1) * d] for n in range(N_MOD)]
        j = i // 2
        gain_mix = raw['norm_mix'][i][None]
        if i % 2 == 0:
            x, s_new, wkv_new, gla_new = _even_layer(x, m6[:3], gain_mix, shift_bufs[j], wkv_states[j],
                                                     gla_states[j], prm['even'][j])
            shifts.append(s_new)
            wkvs.append(wkv_new)
            glas.append(gla_new)
        else:
            x, k_new, v_new, ki_new = _odd_layer(x, m6[:3], gain_mix, k_caches[j], v_caches[j], ki_caches[j],
                                                 raw['rel_bias'], prm['odd'][j])
            ks.append(k_new)
            vs.append(v_new)
            kis.append(ki_new)
        f = prm['ffn'][i]
        x, f_buf = _ffn(x, raw['norm_ffn'][i][None], m6[3], m6[4], m6[5], ffn_bufs[i], f['w_up'], f['conv_w'],
                        f['conv_b'], f['w_down'], raw['norm_final'][None], min(FFN_TOKEN_TILE, t),
                        final=(i == DEPTH - 1))
        ffns.append(f_buf)
    return (x, jnp.stack(shifts), jnp.stack(wkvs), jnp.stack(glas), jnp.stack(ks), jnp.stack(vs),
            jnp.stack(kis), jnp.stack(ffns))


def kernel(x_prompt, x_sample, c_prompt, c_sample, state_shift_ab, state_wkv, state_gla, cache_k, cache_v, cache_kidx, state_ffn_conv, w_ada, b_ada, norm_mix, norm_ffn, norm_final, w_in_ab, mu_a, w0, w_decay_up, a0, w_iclr_up, w_gate_up, k_k, k_a, r_k, gn_a_g, gn_a_b, alpha_up, alpha_b, gn_b_g, w_out_ab, w_in_c, rel_bias, w_out_c, w_ffn_up, ffn_conv_w, ffn_conv_b, w_ffn_down):
    raw = {
        'norm_mix': norm_mix, 'norm_ffn': norm_ffn, 'norm_final': norm_final, 'rel_bias': rel_bias,
        'w_in_ab': w_in_ab, 'mu_a': mu_a, 'w0': w0, 'w_decay_up': w_decay_up, 'a0': a0, 'w_iclr_up': w_iclr_up,
        'w_gate_up': w_gate_up, 'k_k': k_k, 'k_a': k_a, 'r_k': r_k, 'gn_a_g': gn_a_g, 'gn_a_b': gn_a_b,
        'alpha_up': alpha_up, 'alpha_b': alpha_b, 'gn_b_g': gn_b_g, 'w_out_ab': w_out_ab, 'w_in_c': w_in_c,
        'w_out_c': w_out_c, 'w_ffn_up': w_ffn_up, 'ffn_conv_w': ffn_conv_w, 'ffn_conv_b': ffn_conv_b,
        'w_ffn_down': w_ffn_down,
    }
    prm = _prep_params(raw)
    n_even = w_in_ab.shape[0]
    n_odd = w_in_c.shape[0]
    dt = x_prompt.dtype
    bp = x_prompt.shape[0]
    mod = _ada(jnp.concatenate([c_prompt, c_sample], axis=0), w_ada, b_ada)
    mod_p, mod_s = mod[:, :bp], mod[:, bp:]
    out_p = _run_group(
        x_prompt, mod_p,
        jnp.zeros((n_even, bp, 1, A_COLS), dt),
        jnp.zeros((n_even, bp, A_HEADS, A_HEAD_DIM, A_HEAD_DIM), dt),
        jnp.zeros((n_even, bp, B_HEADS, B_KEY_DIM, B_VAL_DIM), dt),
        jnp.zeros((n_odd, bp, 0, C_HEADS, C_HEAD_DIM), dt),
        jnp.zeros((n_odd, bp, 0, C_HEADS, C_HEAD_DIM), dt),
        jnp.zeros((n_odd, bp, 0, IDX_DIM), dt),
        jnp.zeros((DEPTH, bp, CONV_W - 1, 2 * D_FF), dt),
        raw, prm)
    out_s = _run_group(x_sample, mod_s, state_shift_ab, state_wkv, state_gla, cache_k, cache_v, cache_kidx,
                       state_ffn_conv, raw, prm)
    res = []
    for a, b in zip(out_p, out_s):
        res.extend([a, b])
    return tuple(res)
```
